```python
import math
import jax, jax.numpy as jnp
from jax import lax
import numpy as np

D_MODEL = 2048
BATCH = 2
SEQ = 4096
DEPTH = 4

CHUNK = 64

SSD_EXPAND = 2
D_INNER = SSD_EXPAND * D_MODEL
SSD_HEAD_DIM = 64
SSD_HEADS = D_INNER // SSD_HEAD_DIM
SSD_GROUPS = 8
SSD_HEADS_PER_GROUP = SSD_HEADS // SSD_GROUPS
SSD_STATE = 128
SSD_CONV = 4
SSD_CONV_DIM = D_INNER + 2 * SSD_GROUPS * SSD_STATE
SSD_IN_DIM = D_INNER + SSD_CONV_DIM + SSD_HEADS
SSD_CHUNK = CHUNK
GATED_NORM_EPS = 1e-5

SB_HEADS = 16
SB_HEAD_DIM = D_MODEL // SB_HEADS
SB_BLOCK = 128

FFN_DIM = ((8 * D_MODEL // 3 + 255) // 256) * 256
N_EXPERTS = 8
TOP_K = 2
EXPERT_DIM = FFN_DIM

RMS_EPS = 1e-6
N_SSD_LAYERS = (DEPTH + 1) // 2
N_SB_LAYERS = DEPTH // 2

kernel_name = 'hybrid_ssd_stickbreak_moe'


def rms_norm(x, w, eps=RMS_EPS):
    xf = x.astype(jnp.float32)
    y = xf * lax.rsqrt(jnp.mean(xf * xf, axis=-1, keepdims=True) + eps)
    return (y * w.astype(jnp.float32)).astype(x.dtype)


def causal_depthwise_conv(x, w, b):
    c = x.shape[-1]
    y = lax.conv_general_dilated(
        x, w[:, None, :].astype(x.dtype), window_strides=(1,),
        padding=[(SSD_CONV - 1, 0)], dimension_numbers=('NWC', 'WIO', 'NWC'),
        feature_group_count=c)
    return y + b.astype(x.dtype)


def ssd_scan(xs, dt, a_head, bm, cm):
    b, L = xs.shape[0], xs.shape[1]
    c, l = L // SSD_CHUNK, SSD_CHUNK
    G, E, P, N = SSD_GROUPS, SSD_HEADS_PER_GROUP, SSD_HEAD_DIM, SSD_STATE
    x = (xs.astype(jnp.float32) * dt[..., None]).reshape(b, c, l, G, E, P)
    bc = bm.astype(jnp.float32).reshape(b, c, l, G, N)
    cc = cm.astype(jnp.float32).reshape(b, c, l, G, N)
    a = (dt * a_head).reshape(b, c, l, G, E)
    a_cs = jnp.cumsum(jnp.transpose(a, (0, 1, 3, 4, 2)), axis=-1)

    seg = a_cs[..., :, None] - a_cs[..., None, :]
    causal = jnp.tril(jnp.ones((l, l), dtype=bool))
    lmat = jnp.exp(jnp.where(causal, seg, -jnp.inf))
    cb = jnp.einsum('bclgn,bcsgn->bcgls', cc, bc)
    m = cb[:, :, :, None] * lmat
    y_diag = jnp.einsum('bcgels,bcsgep->bclgep', m, x)

    decay_states = jnp.exp(a_cs[..., -1:] - a_cs)
    xd = x * jnp.transpose(decay_states, (0, 1, 4, 2, 3))[..., None]
    states = jnp.einsum('bclgn,bclgep->bcgepn', bc, xd)

    chunk_decay = jnp.exp(a_cs[..., -1])

    def step(h, inp):
        st, dec = inp
        return h * dec[..., None, None] + st, h

    init = jnp.zeros((b, G, E, P, N), jnp.float32)
    _, prev = lax.scan(step, init, (jnp.moveaxis(states, 1, 0), jnp.moveaxis(chunk_decay, 1, 0)))
    prev = jnp.moveaxis(prev, 0, 1)

    out_decay = jnp.transpose(jnp.exp(a_cs), (0, 1, 4, 2, 3))[..., None]
    y_off = jnp.einsum('bclgn,bcgepn->bclgep', cc, prev) * out_decay
    return (y_diag + y_off).reshape(b, L, SSD_HEADS, P)


def ssd_mixer(u, w_in, conv_w, conv_b, dt_bias, a_log, d_skip, norm_w, w_out):
    b, L, _ = u.shape
    zxbcdt = u @ w_in
    z = zxbcdt[..., :D_INNER]
    xbc = zxbcdt[..., D_INNER:D_INNER + SSD_CONV_DIM]
    dt_raw = zxbcdt[..., D_INNER + SSD_CONV_DIM:]
    xbc = jax.nn.silu(causal_depthwise_conv(xbc, conv_w, conv_b))
    gn = SSD_GROUPS * SSD_STATE
    xs = xbc[..., :D_INNER].reshape(b, L, SSD_HEADS, SSD_HEAD_DIM)
    bm = xbc[..., D_INNER:D_INNER + gn].reshape(b, L, SSD_GROUPS, SSD_STATE)
    cm = xbc[..., D_INNER + gn:].reshape(b, L, SSD_GROUPS, SSD_STATE)
    dt = jax.nn.softplus(dt_raw.astype(jnp.float32) + dt_bias.astype(jnp.float32))
    a_head = -jnp.exp(a_log.astype(jnp.float32))
    y = ssd_scan(xs, dt, a_head, bm, cm)
    y = y + xs.astype(jnp.float32) * d_skip.astype(jnp.float32)[:, None]
    y = y.reshape(b, L, D_INNER)
    g = (y * jax.nn.silu(z.astype(jnp.float32))).reshape(b, L, SSD_GROUPS, D_INNER // SSD_GROUPS)
    g = g * lax.rsqrt(jnp.mean(g * g, axis=-1, keepdims=True) + GATED_NORM_EPS)
    g = g.reshape(b, L, D_INNER) * norm_w.astype(jnp.float32)
    return g.astype(u.dtype) @ w_out


def stick_breaking_attention(u, w_qkv, w_out):
    b, L, _ = u.shape
    qkv = (u @ w_qkv).reshape(b, L, 3, SB_HEADS, SB_HEAD_DIM)
    q = jnp.transpose(qkv[:, :, 0], (0, 2, 1, 3))
    k = jnp.transpose(qkv[:, :, 1], (0, 2, 1, 3))
    v = jnp.transpose(qkv[:, :, 2], (0, 2, 1, 3))
    scale = 1.0 / math.sqrt(SB_HEAD_DIM)
    outs = []
    for i in range(L // SB_BLOCK):
        start, end = i * SB_BLOCK, (i + 1) * SB_BLOCK
        qb = q[:, :, start:end]
        kb = k[:, :, :end]
        vb = v[:, :, :end]
        z = jnp.einsum('bhqd,bhkd->bhqk', qb, kb).astype(jnp.float32) * scale
        qpos = jnp.arange(start, end)[:, None]
        kpos = jnp.arange(end)[None, :]
        mask = kpos < qpos
        sp = jnp.where(mask, jax.nn.softplus(z), 0.0)
        tail = lax.cumsum(sp, axis=3, reverse=True) - sp
        attn = jnp.where(mask, jnp.exp(jax.nn.log_sigmoid(z) - tail), 0.0)
        outs.append(jnp.einsum('bhqk,bhkd->bhqd', attn.astype(vb.dtype), vb))
    o = jnp.concatenate(outs, axis=2)
    o = jnp.transpose(o, (0, 2, 1, 3)).reshape(b, L, D_MODEL)
    return o @ w_out


def swiglu(u, w_gate, w_up, w_down):
    return (jax.nn.silu(u @ w_gate) * (u @ w_up)) @ w_down


def moe_swiglu(u, w_router, w_gate, w_up, w_down):
    b, L, d = u.shape
    t = u.reshape(b * L, d)
    logits = (t @ w_router).astype(jnp.float32)
    top_v, top_i = lax.top_k(logits, TOP_K)
    gates = jax.nn.softmax(top_v, axis=-1)
    combine = jnp.einsum('tk,tke->te', gates, jax.nn.one_hot(top_i, N_EXPERTS, dtype=jnp.float32))
    combine = combine.astype(t.dtype)
    out = jnp.zeros_like(t)
    for e in range(N_EXPERTS):
        he = jax.nn.silu(t @ w_gate[e]) * (t @ w_up[e])
        out = out + combine[:, e:e + 1] * (he @ w_down[e])
    return out.reshape(b, L, d)


def setup_inputs(seed: int = 0) -> dict:
    key = jax.random.key(seed)
    ks = jax.random.split(key, 24)
    f32 = jnp.float32

    def nrm(k, shape, fan_in):
        return jax.random.normal(k, shape, f32) * fan_in ** -0.5

    x = jax.random.normal(ks[0], (BATCH, SEQ, D_MODEL), f32)
    norm_mix_w = 1.0 + 0.05 * jax.random.normal(ks[1], (DEPTH, D_MODEL), f32)
    norm_ffn_w = 1.0 + 0.05 * jax.random.normal(ks[2], (DEPTH, D_MODEL), f32)
    norm_final_w = 1.0 + 0.05 * jax.random.normal(ks[3], (D_MODEL,), f32)

    ssd_w_in = nrm(ks[4], (N_SSD_LAYERS, D_MODEL, SSD_IN_DIM), D_MODEL)
    ssd_conv_w = nrm(ks[5], (N_SSD_LAYERS, SSD_CONV, SSD_CONV_DIM), SSD_CONV)
    ssd_conv_b = 0.02 * jax.random.normal(ks[6], (N_SSD_LAYERS, SSD_CONV_DIM), f32)
    dt0 = jnp.exp(jax.random.uniform(ks[7], (N_SSD_LAYERS, SSD_HEADS), f32)
                  * (math.log(0.1) - math.log(0.001)) + math.log(0.001))
    dt0 = jnp.maximum(dt0, 1e-4)
    ssd_dt_bias = dt0 + jnp.log(-jnp.expm1(-dt0))
    ssd_A_log = jnp.log(jax.random.uniform(ks[8], (N_SSD_LAYERS, SSD_HEADS), f32, 1.0, 16.0))
    ssd_D = 1.0 + 0.1 * jax.random.normal(ks[9], (N_SSD_LAYERS, SSD_HEADS), f32)
    ssd_norm_w = 1.0 + 0.05 * jax.random.normal(ks[10], (N_SSD_LAYERS, D_INNER), f32)
    ssd_w_out = nrm(ks[11], (N_SSD_LAYERS, D_INNER, D_MODEL), D_INNER)

    sb_w_qkv = nrm(ks[12], (N_SB_LAYERS, D_MODEL, 3 * D_MODEL), D_MODEL)
    sb_w_out = nrm(ks[13], (N_SB_LAYERS, D_MODEL, D_MODEL), D_MODEL)

    ffn_w_gate = nrm(ks[14], (N_SSD_LAYERS, D_MODEL, FFN_DIM), D_MODEL)
    ffn_w_up = nrm(ks[15], (N_SSD_LAYERS, D_MODEL, FFN_DIM), D_MODEL)
    ffn_w_down = nrm(ks[16], (N_SSD_LAYERS, FFN_DIM, D_MODEL), FFN_DIM)

    moe_w_router = nrm(ks[17], (N_SB_LAYERS, D_MODEL, N_EXPERTS), D_MODEL)
    moe_w_gate = nrm(ks[18], (N_SB_LAYERS, N_EXPERTS, D_MODEL, EXPERT_DIM), D_MODEL)
    moe_w_up = nrm(ks[19], (N_SB_LAYERS, N_EXPERTS, D_MODEL, EXPERT_DIM), D_MODEL)
    moe_w_down = nrm(ks[20], (N_SB_LAYERS, N_EXPERTS, EXPERT_DIM, D_MODEL), EXPERT_DIM)

    return {
        'x': x, 'norm_mix_w': norm_mix_w, 'norm_ffn_w': norm_ffn_w, 'norm_final_w': norm_final_w,
        'ssd_w_in': ssd_w_in, 'ssd_conv_w': ssd_conv_w, 'ssd_conv_b': ssd_conv_b,
        'ssd_dt_bias': ssd_dt_bias, 'ssd_A_log': ssd_A_log, 'ssd_D': ssd_D,
        'ssd_norm_w': ssd_norm_w, 'ssd_w_out': ssd_w_out,
        'sb_w_qkv': sb_w_qkv, 'sb_w_out': sb_w_out,
        'ffn_w_gate': ffn_w_gate, 'ffn_w_up': ffn_w_up, 'ffn_w_down': ffn_w_down,
        'moe_w_router': moe_w_router, 'moe_w_gate': moe_w_gate, 'moe_w_up': moe_w_up,
        'moe_w_down': moe_w_down,
    }


def reference(x, norm_mix_w, norm_ffn_w, norm_final_w, ssd_w_in, ssd_conv_w, ssd_conv_b,
              ssd_dt_bias, ssd_A_log, ssd_D, ssd_norm_w, ssd_w_out, sb_w_qkv, sb_w_out,
              ffn_w_gate, ffn_w_up, ffn_w_down, moe_w_router, moe_w_gate, moe_w_up,
              moe_w_down):
    h = x
    for i in range(DEPTH):
        j = i // 2
        u = rms_norm(h, norm_mix_w[i])
        if i % 2 == 0:
            h = h + ssd_mixer(u, ssd_w_in[j], ssd_conv_w[j], ssd_conv_b[j], ssd_dt_bias[j],
                              ssd_A_log[j], ssd_D[j], ssd_norm_w[j], ssd_w_out[j])
        else:
            h = h + stick_breaking_attention(u, sb_w_qkv[j], sb_w_out[j])
        u = rms_norm(h, norm_ffn_w[i])
        if i % 2 == 0:
            h = h + swiglu(u, ffn_w_gate[j], ffn_w_up[j], ffn_w_down[j])
        else:
            h = h + moe_swiglu(u, moe_w_router[j], moe_w_gate[j], moe_w_up[j], moe_w_down[j])
    return rms_norm(h, norm_final_w)
```

```python
import functools
import math

import jax
import jax.numpy as jnp
from jax import lax
from jax.experimental import pallas as pl
from jax.experimental.pallas import tpu as pltpu

F32 = jnp.float32
BF16 = jnp.bfloat16

SSD_HEAD_DIM = 64
SSD_GROUPS = 8
SSD_STATE = 128
SSD_CONV = 4
GATED_NORM_EPS = 1e-5
SB_HEAD_DIM = 128
N_EXPERTS = 8
TOP_K = 2
RMS_EPS = 1e-6

LANES = 128
VMEM_LIMIT_BYTES = 56 * 1024 * 1024

SSD_Q = 128


def _cparams(*sem):
    return pltpu.CompilerParams(dimension_semantics=sem,
                                vmem_limit_bytes=VMEM_LIMIT_BYTES)


def _rmsnorm_rows(x, w):
    ms = jnp.mean(x * x, axis=-1, keepdims=True)
    return x * lax.rsqrt(ms + RMS_EPS) * w


def _rmsnorm_kernel(h_ref, w_ref, o_ref):
    o_ref[...] = _rmsnorm_rows(h_ref[...], w_ref[...]).astype(o_ref.dtype)


def rmsnorm(h, w, out_dtype, tm=512):
    t, d = h.shape
    return pl.pallas_call(
        _rmsnorm_kernel,
        grid=(t // tm,),
        in_specs=[pl.BlockSpec((tm, d), lambda i: (i, 0)),
                  pl.BlockSpec((1, d), lambda i: (0, 0))],
        out_specs=pl.BlockSpec((tm, d), lambda i: (i, 0)),
        out_shape=jax.ShapeDtypeStruct((t, d), out_dtype),
        compiler_params=_cparams("parallel"),
        name="rmsnorm",
    )(h, w.reshape(1, d))


def _split3(x):
    hi = x.astype(BF16)
    r1 = x - hi.astype(F32)
    mid = r1.astype(BF16)
    lo = (r1 - mid.astype(F32)).astype(BF16)
    return hi, mid, lo


def _dot_f32(a, b):
    a0, a1, a2 = _split3(a)
    b0, b1, b2 = _split3(b)
    d = functools.partial(jnp.dot, preferred_element_type=F32)
    return (d(a0, b0) + (d(a0, b1) + d(a1, b0))
            + (d(a0, b2) + d(a1, b1) + d(a2, b0)))


def _rmsnorm_router_kernel(h_ref, w_ref, wr_ref, u_ref, gate_ref, idx_ref):
    u = _rmsnorm_rows(h_ref[...], w_ref[...])
    u_ref[...] = u.astype(u_ref.dtype)
    logits = _dot_f32(u, wr_ref[...])
    lane = lax.broadcasted_iota(jnp.int32, logits.shape, 1)
    lane_f = lane.astype(F32)
    neg = jnp.float32(-jnp.inf)
    logits = jnp.where(lane < N_EXPERTS, logits, neg)
    m1 = jnp.max(logits, axis=-1, keepdims=True)
    i1 = jnp.min(jnp.where(logits == m1, lane_f, float(LANES)), axis=-1, keepdims=True)
    rest = jnp.where(lane_f == i1, neg, logits)
    m2 = jnp.max(rest, axis=-1, keepdims=True)
    i2 = jnp.min(jnp.where(rest == m2, lane_f, float(LANES)), axis=-1, keepdims=True)
    e2 = jnp.exp(m2 - m1)
    g1 = 1.0 / (1.0 + e2)
    g2 = e2 * g1
    gate_ref[...] = jnp.where(lane == 0, g1, jnp.where(lane == 1, g2, 0.0))
    idx_ref[...] = jnp.where(lane == 0, i1, jnp.where(lane == 1, i2, 0.0)).astype(jnp.int32)


def rmsnorm_router(h, w, w_router, tm=256):
    t, d = h.shape
    wr = jnp.zeros((d, LANES), F32).at[:, :N_EXPERTS].set(w_router)
    u, gates, idx = pl.pallas_call(
        _rmsnorm_router_kernel,
        grid=(t // tm,),
        in_specs=[pl.BlockSpec((tm, d), lambda i: (i, 0)),
                  pl.BlockSpec((1, d), lambda i: (0, 0)),
                  pl.BlockSpec((d, LANES), lambda i: (0, 0))],
        out_specs=[pl.BlockSpec((tm, d), lambda i: (i, 0)),
                   pl.BlockSpec((tm, LANES), lambda i: (i, 0)),
                   pl.BlockSpec((tm, LANES), lambda i: (i, 0))],
        out_shape=[jax.ShapeDtypeStruct((t, d), F32),
                   jax.ShapeDtypeStruct((t, LANES), F32),
                   jax.ShapeDtypeStruct((t, LANES), jnp.int32)],
        compiler_params=_cparams("parallel"),
        name="rmsnorm_router",
    )(h, w.reshape(1, d), wr)
    return u, gates[:, :TOP_K], idx[:, :TOP_K]


def _mm_kernel(a_ref, w_ref, o_ref):
    acc = jnp.dot(a_ref[...].astype(BF16), w_ref[...].astype(BF16),
                  preferred_element_type=F32)
    o_ref[...] = acc.astype(o_ref.dtype)


def _mm_res_kernel(a_ref, w_ref, r_ref, o_ref):
    acc = jnp.dot(a_ref[...].astype(BF16), w_ref[...].astype(BF16),
                  preferred_element_type=F32)
    o_ref[...] = r_ref[...] + acc


def _mm_swiglu_kernel(a_ref, wg_ref, wu_ref, o_ref):
    a = a_ref[...].astype(BF16)
    g = jnp.dot(a, wg_ref[...].astype(BF16), preferred_element_type=F32)
    u = jnp.dot(a, wu_ref[...].astype(BF16), preferred_element_type=F32)
    o_ref[...] = (g * (1.0 / (1.0 + jnp.exp(-g))) * u).astype(o_ref.dtype)


def matmul(a, w, layer, out_dtype, tm, tn, residual=None):
    m, k = a.shape
    n = w.shape[2]
    in_specs = [pl.BlockSpec((tm, k), lambda i, j: (i, 0)),
                pl.BlockSpec((None, k, tn), lambda i, j: (layer, 0, j))]
    args = [a, w]
    body = _mm_kernel
    if residual is not None:
        in_specs.append(pl.BlockSpec((tm, tn), lambda i, j: (i, j)))
        args.append(residual)
        body = _mm_res_kernel
    return pl.pallas_call(
        body,
        grid=(m // tm, n // tn),
        in_specs=in_specs,
        out_specs=pl.BlockSpec((tm, tn), lambda i, j: (i, j)),
        out_shape=jax.ShapeDtypeStruct((m, n), out_dtype),
        compiler_params=_cparams("parallel", "arbitrary"),
        name="matmul",
    )(*args)


def matmul_swiglu(a, w_gate, w_up, layer, tm, tn):
    m, k = a.shape
    n = w_gate.shape[2]
    return pl.pallas_call(
        _mm_swiglu_kernel,
        grid=(m // tm, n // tn),
        in_specs=[pl.BlockSpec((tm, k), lambda i, j: (i, 0)),
                  pl.BlockSpec((None, k, tn), lambda i, j: (layer, 0, j)),
                  pl.BlockSpec((None, k, tn), lambda i, j: (layer, 0, j))],
        out_specs=pl.BlockSpec((tm, tn), lambda i, j: (i, j)),
        out_shape=jax.ShapeDtypeStruct((m, n), BF16),
        compiler_params=_cparams("parallel", "arbitrary"),
        name="matmul_swiglu",
    )(a, w_gate, w_up)


def _mm_tout_kernel(a_ref, w_ref, o_ref):
    acc = lax.dot_general(w_ref[...].astype(BF16), a_ref[...],
                          (((0,), (1,)), ((), ())), preferred_element_type=F32)
    o_ref[...] = acc


def matmul_transposed_out(a, w, layer, tm, tn):
    m, k = a.shape
    n = w.shape[2]
    return pl.pallas_call(
        _mm_tout_kernel,
        grid=(m // tm, pl.cdiv(n, tn)),
        in_specs=[pl.BlockSpec((tm, k), lambda i, j: (i, 0)),
                  pl.BlockSpec((None, k, tn), lambda i, j: (layer, 0, j))],
        out_specs=pl.BlockSpec((tn, tm), lambda i, j: (j, i)),
        out_shape=jax.ShapeDtypeStruct((n, m), F32),
        compiler_params=_cparams("parallel", "arbitrary"),
        name="matmul_transposed_out",
    )(a, w)


def _mm_tin_res_kernel(at_ref, w_ref, r_ref, o_ref):
    acc = lax.dot_general(at_ref[...], w_ref[...].astype(BF16),
                          (((0,), (0,)), ((), ())), preferred_element_type=F32)
    o_ref[...] = r_ref[...] + acc


def matmul_transposed_in_res(at, w, layer, residual, tm, tn):
    k, m = at.shape
    n = w.shape[2]
    return pl.pallas_call(
        _mm_tin_res_kernel,
        grid=(m // tm, n // tn),
        in_specs=[pl.BlockSpec((k, tm), lambda i, j: (0, i)),
                  pl.BlockSpec((None, k, tn), lambda i, j: (layer, 0, j)),
                  pl.BlockSpec((tm, tn), lambda i, j: (i, j))],
        out_specs=pl.BlockSpec((tm, tn), lambda i, j: (i, j)),
        out_shape=jax.ShapeDtypeStruct((m, n), F32),
        compiler_params=_cparams("parallel", "arbitrary"),
        name="matmul_transposed_in_res",
    )(at, w, residual)


def _silu(x):
    return x * (1.0 / (1.0 + jnp.exp(-x)))


def _ssd_kernel(z_ref, x_ref, b_ref, c_ref, dt_ref, cw_ref, hp_ref, cp_ref,
                o_ref,
                xc_ref, yg_ref, tail_ref, state_ref, acs_ref, dts_ref, wgt_ref,
                od_ref, cdec_ref, *, tb, d_inner):
    q = SSD_Q
    p = SSD_HEAD_DIM
    n = SSD_STATE
    n_heads = d_inner // p
    heads_per_group = n_heads // SSD_GROUPS
    gn = SSD_GROUPS * n

    @pl.when(pl.program_id(1) == 0)
    def _():
        tail_ref[...] = jnp.zeros_like(tail_ref)
        state_ref[...] = jnp.zeros_like(state_ref)

    rows = 256

    def conv_rows(src_ref, row0, nrows):
        def body(i, carry):
            r_src = pl.multiple_of(i * rows, rows)
            r_dst = pl.multiple_of(row0 + i * rows, rows)
            cur = src_ref[pl.ds(r_src, rows), :]
            full = jnp.concatenate([tail_ref[pl.ds(r_dst, rows), :], cur], axis=1)
            cw = cw_ref[pl.ds(r_dst, rows), :]
            acc = cw[:, 4:5] + cw[:, 3:4] * cur
            for k in range(SSD_CONV - 1):
                sh = pltpu.roll(full, SSD_CONV - 1 - k, 1)[:, LANES:]
                acc = acc + cw[:, k:k + 1] * sh
            xc_ref[pl.ds(r_dst, rows), :] = _silu(acc)
            tail_ref[pl.ds(r_dst, rows), :] = cur[:, tb - LANES:]
            return carry
        lax.fori_loop(0, nrows // rows, body, 0)

    conv_rows(x_ref, 0, d_inner)
    conv_rows(b_ref, d_inner, gn)
    conv_rows(c_ref, d_inner + gn, gn)

    hp = hp_ref[...]
    dt_bias = hp[:, 0:1]
    a_head = -jnp.exp(hp[:, 1:2])
    ri = lax.broadcasted_iota(jnp.int32, (q, q), 0)
    ci = lax.broadcasted_iota(jnp.int32, (q, q), 1)
    upper_incl = (ri <= ci)
    cum_mat = upper_incl.astype(BF16)

    for sub in range(tb // q):
        cols = slice(sub * q, (sub + 1) * q)
        raw = dt_ref[:, cols] + dt_bias
        dt = jnp.maximum(raw, 0.0) + jnp.log1p(jnp.exp(-jnp.abs(raw)))
        a = dt * a_head
        a0, a1, a2 = _split3(a)
        acs = (jnp.dot(a0, cum_mat, preferred_element_type=F32)
               + jnp.dot(a1, cum_mat, preferred_element_type=F32)
               + jnp.dot(a2, cum_mat, preferred_element_type=F32))
        last = acs[:, q - 1:q]
        acs_ref[...] = acs
        dts_ref[...] = dt
        wgt_ref[...] = jnp.exp(last - acs) * dt
        od_ref[...] = jnp.exp(acs)
        cdec_ref[...] = jnp.exp(jnp.broadcast_to(last, (n_heads, n)))
        d_skip = hp[:, 2:3]

        def group_body(g, carry):
            rb = pl.multiple_of(d_inner + g * n, n)
            rc = pl.multiple_of(d_inner + gn + g * n, n)
            bt = xc_ref[pl.ds(rb, n), cols]
            ct = xc_ref[pl.ds(rc, n), cols]
            bg = bt.T.astype(BF16)
            cbt = jnp.dot(bg, ct.astype(BF16), preferred_element_type=F32)

            def head_body(e, carry2):
                h = g * heads_per_group + e
                r0 = pl.multiple_of(h * p, p)
                acs_row = acs_ref[pl.ds(h, 1), :]
                row_b = jnp.broadcast_to(acs_row, (q, q))
                seg = row_b - row_b.T
                lt = jnp.exp(jnp.where(upper_incl, seg, -jnp.inf))
                mt = (cbt * lt).astype(BF16)
                xt = xc_ref[pl.ds(r0, p), cols]
                xdt = (xt * dts_ref[pl.ds(h, 1), :]).astype(BF16)
                prev = state_ref[h]
                odct = (ct * od_ref[pl.ds(h, 1), :]).astype(BF16)
                yt = (jnp.dot(xdt, mt, preferred_element_type=F32)
                      + jnp.dot(prev.astype(BF16), odct, preferred_element_type=F32))
                dsk = hp_ref[pl.ds(h, 1), :][:, 2:3]
                yt = yt + xt * dsk
                zt = z_ref[pl.ds(r0, p), cols]
                yg_ref[pl.ds(r0, p), cols] = yt * _silu(zt)
                xw = (xt * wgt_ref[pl.ds(h, 1), :]).astype(BF16)
                state_ref[h] = (prev * cdec_ref[pl.ds(h, 1), :]
                                + jnp.dot(xw, bg, preferred_element_type=F32))
                return carry2

            lax.fori_loop(0, heads_per_group, head_body, 0)
            return carry

        lax.fori_loop(0, SSD_GROUPS, group_body, 0)
        del d_skip

    gsz = d_inner // SSD_GROUPS

    def norm_body(g, carry):
        r0 = pl.multiple_of(g * gsz, gsz)
        y = yg_ref[pl.ds(r0, gsz), :]
        ms = jnp.mean(y * y, axis=0, keepdims=True)
        y = y * lax.rsqrt(ms + GATED_NORM_EPS) * cp_ref[pl.ds(r0, gsz), :]
        o_ref[pl.ds(r0, gsz), :] = y.astype(o_ref.dtype)
        return carry

    lax.fori_loop(0, SSD_GROUPS, norm_body, 0)


def ssd_core(zx_t, conv_w, conv_b, dt_bias, a_log, d_skip, norm_w, batch, seq, tb=256):
    n_heads = dt_bias.shape[0]
    d_inner = n_heads * SSD_HEAD_DIM
    gn = SSD_GROUPS * SSD_STATE
    conv_dim = d_inner + 2 * gn
    t = zx_t.shape[1]
    nblk = seq // tb
    cw = jnp.zeros((conv_dim, 8), F32).at[:, :SSD_CONV].set(conv_w.T).at[:, 4].set(conv_b)
    hp = jnp.zeros((n_heads, 8), F32).at[:, 0].set(dt_bias).at[:, 1].set(a_log).at[:, 2].set(d_skip)
    cp = norm_w.reshape(d_inner, 1)

    def tmap(row_block):
        return lambda b, i: (row_block, b * nblk + i)

    const = lambda b, i: (0, 0)
    kern = functools.partial(_ssd_kernel, tb=tb, d_inner=d_inner)
    return pl.pallas_call(
        kern,
        grid=(batch, nblk),
        in_specs=[pl.BlockSpec((d_inner, tb), tmap(0)),
                  pl.BlockSpec((d_inner, tb), tmap(1)),
                  pl.BlockSpec((gn, tb), tmap(2 * d_inner // gn)),
                  pl.BlockSpec((gn, tb), tmap(2 * d_inner // gn + 1)),
                  pl.BlockSpec((n_heads, tb), tmap((2 * d_inner + 2 * gn) // n_heads)),
                  pl.BlockSpec((conv_dim, 8), const),
                  pl.BlockSpec((n_heads, 8), const),
                  pl.BlockSpec((d_inner, 1), const)],
        out_specs=pl.BlockSpec((d_inner, tb), tmap(0)),
        out_shape=jax.ShapeDtypeStruct((d_inner, t), BF16),
        scratch_shapes=[pltpu.VMEM((conv_dim, tb), F32),
                        pltpu.VMEM((d_inner, tb), F32),
                        pltpu.VMEM((conv_dim, LANES), F32),
                        pltpu.VMEM((n_heads, SSD_HEAD_DIM, SSD_STATE), F32),
                        pltpu.VMEM((n_heads, SSD_Q), F32),
                        pltpu.VMEM((n_heads, SSD_Q), F32),
                        pltpu.VMEM((n_heads, SSD_Q), F32),
                        pltpu.VMEM((n_heads, SSD_Q), F32),
                        pltpu.VMEM((n_heads, SSD_STATE), F32)],
        compiler_params=_cparams("arbitrary", "arbitrary"),
        name="ssd_core",
    )(zx_t, zx_t, zx_t, zx_t, zx_t, cw, hp, cp)


def _sb_kernel(q_ref, k_ref, v_ref, o_ref, *, blk, scale):
    qi = pl.program_id(2)
    qb = q_ref[...]
    ri = lax.broadcasted_iota(jnp.int32, (blk, blk), 0)
    ci = lax.broadcasted_iota(jnp.int32, (blk, blk), 1)
    strict_lower = (ri > ci)
    tail_mat = strict_lower.astype(BF16)

    def tile(kb, carry, acc, masked):
        k0 = pl.multiple_of(kb * blk, blk)
        kt = k_ref[pl.ds(k0, blk), :]
        vt = v_ref[pl.ds(k0, blk), :]
        z = lax.dot_general(qb, kt, (((1,), (1,)), ((), ())),
                            preferred_element_type=F32) * scale
        sp = jnp.maximum(z, 0.0) + jnp.log(1.0 + jnp.exp(-jnp.abs(z)))
        if masked:
            sp = jnp.where(strict_lower, sp, 0.0)
        tail = jnp.dot(sp.astype(BF16), tail_mat, preferred_element_type=F32) + carry
        w = jnp.exp(z - sp - tail)
        if masked:
            w = jnp.where(strict_lower, w, 0.0)
        acc = acc + jnp.dot(w.astype(BF16), vt, preferred_element_type=F32)
        carry = carry + jnp.sum(sp, axis=1, keepdims=True)
        return carry, acc

    carry = jnp.zeros((blk, 1), F32)
    acc = jnp.zeros((blk, q_ref.shape[1]), F32)
    carry, acc = tile(qi, carry, acc, True)

    def body(i, ca):
        return tile(qi - 1 - i, ca[0], ca[1], False)

    carry, acc = lax.fori_loop(0, qi, body, (carry, acc))
    o_ref[...] = acc.astype(o_ref.dtype)


def sb_attention(qkv, batch, seq, n_heads, blk=256):
    t = qkv.shape[0]
    dh = SB_HEAD_DIM
    nq = seq // blk
    kern = functools.partial(_sb_kernel, blk=blk, scale=1.0 / math.sqrt(dh))
    return pl.pallas_call(
        kern,
        grid=(batch, n_heads, nq),
        in_specs=[pl.BlockSpec((blk, dh), lambda b, h, i: (b * nq + i, h)),
                  pl.BlockSpec((seq, dh), lambda b, h, i: (b, n_heads + h)),
                  pl.BlockSpec((seq, dh), lambda b, h, i: (b, 2 * n_heads + h))],
        out_specs=pl.BlockSpec((blk, dh), lambda b, h, i: (b * nq + i, h)),
        out_shape=jax.ShapeDtypeStruct((t, n_heads * dh), BF16),
        compiler_params=_cparams("parallel", "parallel", "arbitrary"),
        name="sb_attention",
    )(qkv, qkv, qkv)


def _gather_rows_kernel(idx_ref, src_ref, o_ref, sem, *, rows):
    base = pl.program_id(0) * rows

    def row_copy(r, src_row):
        return pltpu.make_async_copy(src_ref.at[pl.ds(src_row, 1), :],
                                     o_ref.at[pl.ds(r, 1), :], sem)

    def issue(r, c):
        row_copy(r, idx_ref[base + r]).start()
        return c

    def drain(r, c):
        row_copy(r, 0).wait()
        return c

    lax.fori_loop(0, rows, issue, 0)
    lax.fori_loop(0, rows, drain, 0)


def gather_rows(src, idx, rows=256):
    n = idx.shape[0]
    d = src.shape[1]
    return pl.pallas_call(
        functools.partial(_gather_rows_kernel, rows=rows),
        grid_spec=pltpu.PrefetchScalarGridSpec(
            num_scalar_prefetch=1,
            grid=(n // rows,),
            in_specs=[pl.BlockSpec(memory_space=pl.ANY)],
            out_specs=pl.BlockSpec((rows, d), lambda i, idx: (i, 0)),
            scratch_shapes=[pltpu.SemaphoreType.DMA(())]),
        out_shape=jax.ShapeDtypeStruct((n, d), src.dtype),
        compiler_params=_cparams("arbitrary"),
        name="gather_rows",
    )(idx, src)


def _moe_up_kernel(be_ref, nb_ref, x_ref, wg_ref, wu_ref, o_ref):
    i = pl.program_id(1)

    @pl.when(i < nb_ref[0])
    def _():
        a = x_ref[...].astype(BF16)
        g = jnp.dot(a, wg_ref[...].astype(BF16), preferred_element_type=F32)
        u = jnp.dot(a, wu_ref[...].astype(BF16), preferred_element_type=F32)
        o_ref[...] = (g * (1.0 / (1.0 + jnp.exp(-g))) * u).astype(o_ref.dtype)

    @pl.when(i >= nb_ref[0])
    def _():
        o_ref[...] = jnp.zeros_like(o_ref)


def moe_up(xs, w_gate, w_up, layer, block_expert, n_used, tm, tn):
    pr, k = xs.shape
    f = w_gate.shape[3]
    return pl.pallas_call(
        _moe_up_kernel,
        grid_spec=pltpu.PrefetchScalarGridSpec(
            num_scalar_prefetch=2,
            grid=(f // tn, pr // tm),
            in_specs=[pl.BlockSpec((tm, k), lambda j, i, be, nb: (i, 0)),
                      pl.BlockSpec((None, None, k, tn),
                                   lambda j, i, be, nb: (layer, be[i], 0, j)),
                      pl.BlockSpec((None, None, k, tn),
                                   lambda j, i, be, nb: (layer, be[i], 0, j))],
            out_specs=pl.BlockSpec((tm, tn), lambda j, i, be, nb: (i, j))),
        out_shape=jax.ShapeDtypeStruct((pr, f), BF16),
        compiler_params=_cparams("parallel", "arbitrary"),
        name="moe_up",
    )(block_expert, n_used, xs, w_gate, w_up)


def _moe_down_kernel(be_ref, nb_ref, h_ref, w_ref, o_ref):
    i = pl.program_id(1)

    @pl.when(i < nb_ref[0])
    def _():
        o_ref[...] = jnp.dot(h_ref[...], w_ref[...].astype(BF16),
                             preferred_element_type=F32)

    @pl.when(i >= nb_ref[0])
    def _():
        o_ref[...] = jnp.zeros_like(o_ref)


def moe_down(hid, w_down, layer, block_expert, n_used, tm, tn):
    pr, f = hid.shape
    d = w_down.shape[3]
    return pl.pallas_call(
        _moe_down_kernel,
        grid_spec=pltpu.PrefetchScalarGridSpec(
            num_scalar_prefetch=2,
            grid=(d // tn, pr // tm),
            in_specs=[pl.BlockSpec((tm, f), lambda j, i, be, nb: (i, 0)),
                      pl.BlockSpec((None, None, f, tn),
                                   lambda j, i, be, nb: (layer, be[i], 0, j))],
            out_specs=pl.BlockSpec((tm, tn), lambda j, i, be, nb: (i, j))),
        out_shape=jax.ShapeDtypeStruct((pr, d), F32),
        compiler_params=_cparams("parallel", "arbitrary"),
        name="moe_down",
    )(block_expert, n_used, hid, w_down)


def _moe_combine_kernel(pos_ref, y_ref, h_ref, g_ref, o_ref, buf_ref, sem, *, rows):
    base = pl.program_id(0) * rows

    def row_copy(r, src_row):
        return pltpu.make_async_copy(y_ref.at[pl.ds(src_row, 1), :],
                                     buf_ref.at[pl.ds(r, 1), :], sem)

    def issue(r, c):
        for k in range(TOP_K):
            row_copy(k * rows + r, pos_ref[TOP_K * (base + r) + k]).start()
        return c

    def drain(r, c):
        row_copy(r, 0).wait()
        return c

    lax.fori_loop(0, rows, issue, 0)
    lax.fori_loop(0, TOP_K * rows, drain, 0)
    g = g_ref[...]
    acc = h_ref[...]
    for k in range(TOP_K):
        acc = acc + g[:, k:k + 1] * buf_ref[pl.ds(k * rows, rows), :]
    o_ref[...] = acc


def moe_combine(y, pos, h, gates, rows=128):
    t, d = h.shape
    return pl.pallas_call(
        functools.partial(_moe_combine_kernel, rows=rows),
        grid_spec=pltpu.PrefetchScalarGridSpec(
            num_scalar_prefetch=1,
            grid=(t // rows,),
            in_specs=[pl.BlockSpec(memory_space=pl.ANY),
                      pl.BlockSpec((rows, d), lambda i, pos: (i, 0)),
                      pl.BlockSpec((rows, TOP_K), lambda i, pos: (i, 0))],
            out_specs=pl.BlockSpec((rows, d), lambda i, pos: (i, 0)),
            scratch_shapes=[pltpu.VMEM((TOP_K * rows, d), F32),
                            pltpu.SemaphoreType.DMA(())]),
        out_shape=jax.ShapeDtypeStruct((t, d), F32),
        compiler_params=_cparams("arbitrary"),
        name="moe_combine",
    )(pos, y, h, gates)


def moe_routing(idx, tm):
    t = idx.shape[0]
    n_pairs = t * TOP_K
    nb = n_pairs // tm + N_EXPERTS
    e_flat = idx.reshape(n_pairs)
    onehot = (e_flat[:, None] == jnp.arange(N_EXPERTS)[None, :]).astype(jnp.int32)
    ranks = jnp.cumsum(onehot, axis=0) - onehot
    counts = jnp.sum(onehot, axis=0)
    blocks = (counts + tm - 1) // tm
    blk_end = jnp.cumsum(blocks)
    blk_start = blk_end - blocks
    rank = jnp.sum(ranks * onehot, axis=1)
    pos = blk_start[e_flat] * tm + rank
    src_token = jnp.zeros((nb * tm,), jnp.int32).at[pos].set(
        jnp.arange(n_pairs, dtype=jnp.int32) // TOP_K)
    n_used = blk_end[-1]
    bid = jnp.arange(nb)
    be = jnp.sum((bid[:, None] >= blk_end[None, :]).astype(jnp.int32), axis=1)
    last_e = jnp.sum((n_used - 1 >= blk_end).astype(jnp.int32))
    block_expert = jnp.where(bid < n_used, be, last_e).astype(jnp.int32)
    return src_token, pos.astype(jnp.int32), block_expert, n_used.reshape(1).astype(jnp.int32)


def moe_layer(h, norm_w, w_router, w_gate, w_up, w_down, layer, tm=256):
    u, gates, idx = rmsnorm_router(h, norm_w, w_router)
    src_token, pos, block_expert, n_used = moe_routing(idx, tm)
    xs = gather_rows(u, src_token)
    hid = moe_up(xs, w_gate, w_up, layer, block_expert, n_used, tm, 512)
    y = moe_down(hid, w_down, layer, block_expert, n_used, tm, 512)
    return moe_combine(y, pos, h, gates)


def ssd_layer(h, norm_w, w_in, conv_w, conv_b, dt_bias, a_log, d_skip, ssd_norm_w, w_out,
              layer, batch, seq):
    u = rmsnorm(h, norm_w, BF16)
    zx_t = matmul_transposed_out(u, w_in, layer, 1024, 512)
    g_t = ssd_core(zx_t, conv_w, conv_b, dt_bias, a_log, d_skip, ssd_norm_w, batch, seq)
    return matmul_transposed_in_res(g_t, w_out, layer, h, 1024, 512)


def sb_layer(h, norm_w, w_qkv, w_out, layer, batch, seq):
    u = rmsnorm(h, norm_w, BF16)
    qkv = matmul(u, w_qkv, layer, BF16, 1024, 512)
    n_heads = w_out.shape[1] // SB_HEAD_DIM
    o = sb_attention(qkv, batch, seq, n_heads)
    return matmul(o, w_out, layer, F32, 1024, 512, residual=h)


def swiglu_layer(h, norm_w, w_gate, w_up, w_down, layer):
    u = rmsnorm(h, norm_w, BF16)
    hid = matmul_swiglu(u, w_gate, w_up, layer, 1024, 256)
    return matmul(hid, w_down, layer, F32, 1024, 256, residual=h)


def kernel(x, norm_mix_w, norm_ffn_w, norm_final_w, ssd_w_in, ssd_conv_w, ssd_conv_b, ssd_dt_bias, ssd_A_log, ssd_D, ssd_norm_w, ssd_w_out, sb_w_qkv, sb_w_out, ffn_w_gate, ffn_w_up, ffn_w_down, moe_w_router, moe_w_gate, moe_w_up, moe_w_down):
    batch, seq, d = x.shape
    depth = norm_mix_w.shape[0]
    h = x.reshape(batch * seq, d)
    for i in range(depth):
        j = i // 2
        if i % 2 == 0:
            h = ssd_layer(h, norm_mix_w[i], ssd_w_in, ssd_conv_w[j], ssd_conv_b[j],
                          ssd_dt_bias[j], ssd_A_log[j], ssd_D[j], ssd_norm_w[j], ssd_w_out,
                          j, batch, seq)
            h = swiglu_layer(h, norm_ffn_w[i], ffn_w_gate, ffn_w_up, ffn_w_down, j)
        else:
            h = sb_layer(h, norm_mix_w[i], sb_w_qkv, sb_w_out, j, batch, seq)
            h = moe_layer(h, norm_ffn_w[i], moe_w_router[j], moe_w_gate, moe_w_up,
                          moe_w_down, j)
    return rmsnorm(h, norm_final_w, F32).reshape(batch, seq, d)
```

```python
import functools
import math

import jax
import jax.numpy as jnp
from jax import lax
from jax.experimental import pallas as pl
from jax.experimental.pallas import tpu as pltpu

F32 = jnp.float32
BF16 = jnp.bfloat16

SSD_HEAD_DIM = 64
SSD_GROUPS = 8
SSD_STATE = 128
SSD_CONV = 4
GATED_NORM_EPS = 1e-5
SB_HEAD_DIM = 128
N_EXPERTS = 8
TOP_K = 2
RMS_EPS = 1e-6

LANES = 128
SUBLANES = 8
VMEM_LIMIT_BYTES = 56 * 1024 * 1024

SSD_Q = 128
LOG2E = 1.4426950408889634
LN2 = 0.6931471805599453


def _cparams(*sem):
    return pltpu.CompilerParams(dimension_semantics=sem,
                                vmem_limit_bytes=VMEM_LIMIT_BYTES)


def _silu(x):
    return x * (1.0 / (1.0 + jnp.exp(-x)))


def _softplus(x):
    return jnp.maximum(x, 0.0) + jnp.log1p(jnp.exp(-jnp.abs(x)))


def _rmsnorm_rows(x, w):
    ms = jnp.mean(x * x, axis=-1, keepdims=True)
    return x * lax.rsqrt(ms + RMS_EPS) * w


def _rmsnorm_kernel(h_ref, w_ref, o_ref):
    o_ref[...] = _rmsnorm_rows(h_ref[...], w_ref[...]).astype(o_ref.dtype)


def rmsnorm(h, w, out_dtype, tm=512):
    t, d = h.shape
    return pl.pallas_call(
        _rmsnorm_kernel,
        grid=(t // tm,),
        in_specs=[pl.BlockSpec((tm, d), lambda i: (i, 0)),
                  pl.BlockSpec((1, d), lambda i: (0, 0))],
        out_specs=pl.BlockSpec((tm, d), lambda i: (i, 0)),
        out_shape=jax.ShapeDtypeStruct((t, d), out_dtype),
        compiler_params=_cparams("parallel"),
        name="rmsnorm",
    )(h, w.reshape(1, d))


def _split3(x):
    hi = x.astype(BF16)
    r1 = x - hi.astype(F32)
    mid = r1.astype(BF16)
    lo = (r1 - mid.astype(F32)).astype(BF16)
    return hi, mid, lo


def _dot_f32(a, b):
    a0, a1, a2 = _split3(a)
    b0, b1, b2 = _split3(b)
    d = functools.partial(jnp.dot, preferred_element_type=F32)
    return (d(a0, b0) + (d(a0, b1) + d(a1, b0))
            + (d(a0, b2) + d(a1, b1) + d(a2, b0)))


def _rmsnorm_router_kernel(h_ref, w_ref, wr_ref, u_ref, gate_ref, idx_ref):
    u = _rmsnorm_rows(h_ref[...], w_ref[...])
    u_ref[...] = u.astype(u_ref.dtype)
    logits = _dot_f32(u, wr_ref[...])
    lane = lax.broadcasted_iota(jnp.int32, logits.shape, 1)
    lane_f = lane.astype(F32)
    neg = jnp.float32(-jnp.inf)
    logits = jnp.where(lane < N_EXPERTS, logits, neg)
    m1 = jnp.max(logits, axis=-1, keepdims=True)
    i1 = jnp.min(jnp.where(logits == m1, lane_f, float(LANES)), axis=-1, keepdims=True)
    rest = jnp.where(lane_f == i1, neg, logits)
    m2 = jnp.max(rest, axis=-1, keepdims=True)
    i2 = jnp.min(jnp.where(rest == m2, lane_f, float(LANES)), axis=-1, keepdims=True)
    e2 = jnp.exp(m2 - m1)
    g1 = 1.0 / (1.0 + e2)
    g2 = e2 * g1
    gate_ref[...] = jnp.where(lane == 0, g1, jnp.where(lane == 1, g2, 0.0))
    idx_ref[...] = jnp.where(lane == 0, i1, jnp.where(lane == 1, i2, 0.0)).astype(jnp.int32)


def rmsnorm_router(h, w, w_router, tm=256):
    t, d = h.shape
    wr = jnp.zeros((d, LANES), F32).at[:, :N_EXPERTS].set(w_router)
    u, gates, idx = pl.pallas_call(
        _rmsnorm_router_kernel,
        grid=(t // tm,),
        in_specs=[pl.BlockSpec((tm, d), lambda i: (i, 0)),
                  pl.BlockSpec((1, d), lambda i: (0, 0)),
                  pl.BlockSpec((d, LANES), lambda i: (0, 0))],
        out_specs=[pl.BlockSpec((tm, d), lambda i: (i, 0)),
                   pl.BlockSpec((tm, LANES), lambda i: (i, 0)),
                   pl.BlockSpec((tm, LANES), lambda i: (i, 0))],
        out_shape=[jax.ShapeDtypeStruct((t, d), F32),
                   jax.ShapeDtypeStruct((t, LANES), F32),
                   jax.ShapeDtypeStruct((t, LANES), jnp.int32)],
        compiler_params=_cparams("parallel"),
        name="rmsnorm_router",
    )(h, w.reshape(1, d), wr)
    return u, gates[:, :TOP_K], idx[:, :TOP_K]


def _cast_weights_once(w_ref, ws_ref):
    @pl.when(pl.program_id(1) == 0)
    def _():
        ws_ref[...] = w_ref[...].astype(BF16)


def _mm_kernel(a_ref, w_ref, o_ref, ws_ref, *, scaled_tiles, scale):
    _cast_weights_once(w_ref, ws_ref)
    acc = jnp.dot(a_ref[...], ws_ref[...], preferred_element_type=F32)
    if scaled_tiles:
        acc = acc * jnp.where(pl.program_id(0) < scaled_tiles, scale, 1.0)
    o_ref[...] = acc.astype(o_ref.dtype)


def _mm_res_kernel(a_ref, w_ref, r_ref, o_ref, ws_ref):
    _cast_weights_once(w_ref, ws_ref)
    o_ref[...] = r_ref[...] + jnp.dot(a_ref[...], ws_ref[...], preferred_element_type=F32)


def _mm_tin_res_kernel(at_ref, w_ref, r_ref, o_ref, ws_ref):
    _cast_weights_once(w_ref, ws_ref)
    acc = lax.dot_general(at_ref[...], ws_ref[...], (((0,), (0,)), ((), ())),
                          preferred_element_type=F32)
    o_ref[...] = r_ref[...] + acc


def _mm_swiglu_kernel(a_ref, wg_ref, wu_ref, o_ref, wgs_ref, wus_ref):
    _cast_weights_once(wg_ref, wgs_ref)
    _cast_weights_once(wu_ref, wus_ref)
    a = a_ref[...]
    g = jnp.dot(a, wgs_ref[...], preferred_element_type=F32)
    u = jnp.dot(a, wus_ref[...], preferred_element_type=F32)
    o_ref[...] = (_silu(g) * u).astype(o_ref.dtype)


def matmul(a, w, layer, out_dtype, tm, tn, residual=None, transposed_a=False,
           scaled_cols=0, scale=1.0):
    if transposed_a:
        k, m = a.shape
        a_spec = pl.BlockSpec((k, tm), lambda j, i: (0, i))
    else:
        m, k = a.shape
        a_spec = pl.BlockSpec((tm, k), lambda j, i: (i, 0))
    n = w.shape[2]
    in_specs = [a_spec, pl.BlockSpec((None, k, tn), lambda j, i: (layer, 0, j))]
    args = [a, w]
    if residual is not None:
        in_specs.append(pl.BlockSpec((tm, tn), lambda j, i: (i, j)))
        args.append(residual)
        body = _mm_tin_res_kernel if transposed_a else _mm_res_kernel
    else:
        assert not transposed_a and scaled_cols % tn == 0
        body = functools.partial(_mm_kernel, scaled_tiles=scaled_cols // tn, scale=scale)
    return pl.pallas_call(
        body,
        grid=(n // tn, m // tm),
        in_specs=in_specs,
        out_specs=pl.BlockSpec((tm, tn), lambda j, i: (i, j)),
        out_shape=jax.ShapeDtypeStruct((m, n), out_dtype),
        scratch_shapes=[pltpu.VMEM((k, tn), BF16)],
        compiler_params=_cparams("arbitrary", "arbitrary"),
        name="matmul",
    )(*args)


def matmul_swiglu(a, w_gate, w_up, layer, tm, tn):
    m, k = a.shape
    n = w_gate.shape[2]
    w_spec = pl.BlockSpec((None, k, tn), lambda j, i: (layer, 0, j))
    return pl.pallas_call(
        _mm_swiglu_kernel,
        grid=(n // tn, m // tm),
        in_specs=[pl.BlockSpec((tm, k), lambda j, i: (i, 0)), w_spec, w_spec],
        out_specs=pl.BlockSpec((tm, tn), lambda j, i: (i, j)),
        out_shape=jax.ShapeDtypeStruct((m, n), BF16),
        scratch_shapes=[pltpu.VMEM((k, tn), BF16), pltpu.VMEM((k, tn), BF16)],
        compiler_params=_cparams("arbitrary", "arbitrary"),
        name="matmul_swiglu",
    )(a, w_gate, w_up)


INPROJ_ROWS = 256


def _inproj_gate_kernel(a_ref, w_ref, o_ref, ws_ref):
    _cast_weights_once(w_ref, ws_ref)
    rc = INPROJ_ROWS
    for c in range(a_ref.shape[0] // rc):
        z = jnp.dot(a_ref[pl.ds(c * rc, rc), :], ws_ref[...], preferred_element_type=F32)
        o_ref[:, pl.ds(c * rc, rc)] = _silu(z).T.astype(o_ref.dtype)


def _inproj_conv_kernel(a_ref, w_ref, cw_ref, o_ref, ws_ref, halo_ref, pad_ref, *,
                        blocks_per_batch):
    _cast_weights_once(w_ref, ws_ref)
    tm = a_ref.shape[0]
    rc = INPROJ_ROWS
    first = (pl.program_id(1) % blocks_per_batch) == 0
    pad_ref[pl.ds(0, SUBLANES), :] = jnp.where(first, 0.0, halo_ref[...])
    cw = cw_ref[...]
    for c in range(tm // rc):
        acc = jnp.dot(a_ref[pl.ds(c * rc, rc), :], ws_ref[...], preferred_element_type=F32)
        pad_ref[pl.ds(SUBLANES + c * rc, rc), :] = acc
        y = cw[SSD_CONV:SSD_CONV + 1, :] + cw[SSD_CONV - 1:SSD_CONV, :] * acc
        for k in range(SSD_CONV - 1):
            lag = SSD_CONV - 1 - k
            y = y + cw[k:k + 1, :] * pad_ref[pl.ds(SUBLANES + c * rc - lag, rc), :]
        o_ref[:, pl.ds(c * rc, rc)] = _silu(y).T
    halo_ref[...] = pad_ref[pl.ds(tm, SUBLANES), :]


def _inproj_dt_kernel(a_ref, w_ref, b_ref, o_ref, *, n_heads):
    raw = jnp.dot(a_ref[...], w_ref[...].astype(BF16), preferred_element_type=F32)
    dt = _softplus(raw + b_ref[...])
    o_ref[...] = dt.T[:n_heads, :]


def ssd_in_projection(u, w_in, layer, conv_w, conv_b, dt_bias, d_inner, seq, tm=1024, tn=512):
    m, k = u.shape
    conv_dim = conv_w.shape[1]
    n_heads = dt_bias.shape[0]
    a_spec = pl.BlockSpec((tm, k), lambda j, i: (i, 0))
    cparams = _cparams("arbitrary", "arbitrary")

    gz_t = pl.pallas_call(
        _inproj_gate_kernel,
        grid=(d_inner // tn, m // tm),
        in_specs=[a_spec, pl.BlockSpec((None, k, tn), lambda j, i: (layer, 0, j))],
        out_specs=pl.BlockSpec((tn, tm), lambda j, i: (j, i)),
        out_shape=jax.ShapeDtypeStruct((d_inner, m), BF16),
        scratch_shapes=[pltpu.VMEM((k, tn), BF16)],
        compiler_params=cparams,
        name="ssd_inproj_gate",
    )(u, w_in)

    cw = jnp.zeros((SUBLANES, conv_dim), F32).at[:SSD_CONV].set(conv_w).at[SSD_CONV].set(conv_b)
    j0 = d_inner // tn
    xbc_t = pl.pallas_call(
        functools.partial(_inproj_conv_kernel, blocks_per_batch=seq // tm),
        grid=(conv_dim // tn, m // tm),
        in_specs=[a_spec,
                  pl.BlockSpec((None, k, tn), lambda j, i: (layer, 0, j0 + j)),
                  pl.BlockSpec((SUBLANES, tn), lambda j, i: (0, j))],
        out_specs=pl.BlockSpec((tn, tm), lambda j, i: (j, i)),
        out_shape=jax.ShapeDtypeStruct((conv_dim, m), F32),
        scratch_shapes=[pltpu.VMEM((k, tn), BF16),
                        pltpu.VMEM((SUBLANES, tn), F32),
                        pltpu.VMEM((tm + SUBLANES, tn), F32)],
        compiler_params=cparams,
        name="ssd_inproj_conv",
    )(u, w_in, cw)

    bias = jnp.zeros((1, LANES), F32).at[0, :n_heads].set(dt_bias)
    dt_t = pl.pallas_call(
        functools.partial(_inproj_dt_kernel, n_heads=n_heads),
        grid=(m // tm,),
        in_specs=[pl.BlockSpec((tm, k), lambda i: (i, 0)),
                  pl.BlockSpec((None, k, LANES),
                               lambda i: (layer, 0, (d_inner + conv_dim) // LANES)),
                  pl.BlockSpec((1, LANES), lambda i: (0, 0))],
        out_specs=pl.BlockSpec((n_heads, tm), lambda i: (0, i)),
        out_shape=jax.ShapeDtypeStruct((n_heads, m), F32),
        compiler_params=_cparams("parallel"),
        name="ssd_inproj_dt",
    )(u, w_in, bias)
    return gz_t, xbc_t, dt_t


def _ssd_kernel(gz_ref, x_ref, b_ref, c_ref, dt_ref, hp_ref, cp_ref,
                o_ref,
                yg_ref, state_ref, acs_ref, dts_ref, wgt_ref, od_ref, cdec_ref, *, tb, d_inner):
    q = SSD_Q
    p = SSD_HEAD_DIM
    n = SSD_STATE
    n_heads = d_inner // p
    heads_per_group = n_heads // SSD_GROUPS

    @pl.when(pl.program_id(1) == 0)
    def _():
        state_ref[...] = jnp.zeros_like(state_ref)

    hp = hp_ref[...]
    a_head = -jnp.exp(hp[:, 0:1])
    ri = lax.broadcasted_iota(jnp.int32, (q, q), 0)
    ci = lax.broadcasted_iota(jnp.int32, (q, q), 1)
    upper_incl = (ri <= ci)
    cum_mat = upper_incl.astype(BF16)

    for sub in range(tb // q):
        cols = slice(sub * q, (sub + 1) * q)
        dt = dt_ref[:, cols]
        a = dt * a_head
        a0, a1, a2 = _split3(a)
        acs = (jnp.dot(a0, cum_mat, preferred_element_type=F32)
               + jnp.dot(a1, cum_mat, preferred_element_type=F32)
               + jnp.dot(a2, cum_mat, preferred_element_type=F32))
        last = acs[:, q - 1:q]
        acs_ref[...] = acs
        dts_ref[...] = dt
        wgt_ref[...] = jnp.exp(last - acs) * dt
        od_ref[...] = jnp.exp(acs)
        cdec_ref[...] = jnp.exp(jnp.broadcast_to(last, (n_heads, n)))

        def group_body(g, carry):
            rg = pl.multiple_of(g * n, n)
            bt = b_ref[pl.ds(rg, n), cols]
            ct = c_ref[pl.ds(rg, n), cols]
            bg = bt.T.astype(BF16)
            cbt = jnp.dot(bg, ct.astype(BF16), preferred_element_type=F32)

            for e in range(heads_per_group):
                h = g * heads_per_group + e
                r0 = pl.multiple_of(h * p, p)
                acs_row = acs_ref[pl.ds(h, 1), :]
                row_b = jnp.broadcast_to(acs_row, (q, q))
                seg = row_b - row_b.T
                lt = jnp.exp(jnp.where(upper_incl, seg, -jnp.inf))
                mt = (cbt * lt).astype(BF16)
                xt = x_ref[pl.ds(r0, p), cols]
                xdt = (xt * dts_ref[pl.ds(h, 1), :]).astype(BF16)
                prev = state_ref[h]
                odct = (ct * od_ref[pl.ds(h, 1), :]).astype(BF16)
                yt = (jnp.dot(xdt, mt, preferred_element_type=F32)
                      + jnp.dot(prev.astype(BF16), odct, preferred_element_type=F32))
                dsk = hp_ref[pl.ds(h, 1), :][:, 1:2]
                yt = yt + xt * dsk
                yg_ref[pl.ds(r0, p), cols] = yt * gz_ref[pl.ds(r0, p), cols].astype(F32)
                xw = (xt * wgt_ref[pl.ds(h, 1), :]).astype(BF16)
                state_ref[h] = (prev * cdec_ref[pl.ds(h, 1), :]
                                + jnp.dot(xw, bg, preferred_element_type=F32))
            return carry

        lax.fori_loop(0, SSD_GROUPS, group_body, 0)

    gsz = d_inner // SSD_GROUPS

    def norm_body(g, carry):
        r0 = pl.multiple_of(g * gsz, gsz)
        y = yg_ref[pl.ds(r0, gsz), :]
        ms = jnp.mean(y * y, axis=0, keepdims=True)
        y = y * lax.rsqrt(ms + GATED_NORM_EPS) * cp_ref[pl.ds(r0, gsz), :]
        o_ref[pl.ds(r0, gsz), :] = y.astype(o_ref.dtype)
        return carry

    lax.fori_loop(0, SSD_GROUPS, norm_body, 0)


def ssd_core(gz_t, xbc_t, dt_t, a_log, d_skip, norm_w, batch, seq, tb=256):
    n_heads = a_log.shape[0]
    d_inner = n_heads * SSD_HEAD_DIM
    gn = SSD_GROUPS * SSD_STATE
    t = gz_t.shape[1]
    nblk = seq // tb
    hp = jnp.zeros((n_heads, 8), F32).at[:, 0].set(a_log).at[:, 1].set(d_skip)
    cp = norm_w.reshape(d_inner, 1)

    def tmap(row_block):
        return lambda b, i: (row_block, b * nblk + i)

    const = lambda b, i: (0, 0)
    kern = functools.partial(_ssd_kernel, tb=tb, d_inner=d_inner)
    return pl.pallas_call(
        kern,
        grid=(batch, nblk),
        in_specs=[pl.BlockSpec((d_inner, tb), tmap(0)),
                  pl.BlockSpec((d_inner, tb), tmap(0)),
                  pl.BlockSpec((gn, tb), tmap(d_inner // gn)),
                  pl.BlockSpec((gn, tb), tmap(d_inner // gn + 1)),
                  pl.BlockSpec((n_heads, tb), tmap(0)),
                  pl.BlockSpec((n_heads, 8), const),
                  pl.BlockSpec((d_inner, 1), const)],
        out_specs=pl.BlockSpec((d_inner, tb), tmap(0)),
        out_shape=jax.ShapeDtypeStruct((d_inner, t), BF16),
        scratch_shapes=[pltpu.VMEM((d_inner, tb), F32),
                        pltpu.VMEM((n_heads, SSD_HEAD_DIM, SSD_STATE), F32),
                        pltpu.VMEM((n_heads, SSD_Q), F32),
                        pltpu.VMEM((n_heads, SSD_Q), F32),
                        pltpu.VMEM((n_heads, SSD_Q), F32),
                        pltpu.VMEM((n_heads, SSD_Q), F32),
                        pltpu.VMEM((n_heads, SSD_STATE), F32)],
        compiler_params=_cparams("arbitrary", "arbitrary"),
        name="ssd_core",
    )(gz_t, xbc_t, xbc_t, xbc_t, dt_t, hp, cp)


def _sb_kernel(q_ref, k_ref, v_ref, o_ref, acc_ref, *, blk, heads):
    qi = pl.program_id(2)
    dh = SB_HEAD_DIM
    ri = lax.broadcasted_iota(jnp.int32, (blk, blk), 0)
    ci = lax.broadcasted_iota(jnp.int32, (blk, blk), 1)
    strict_lower = (ri > ci)
    tail_mat = strict_lower.astype(BF16)
    sign = jnp.uint32(0x80000000)

    def tile(kb, carries, masked):
        k0 = pl.multiple_of(kb * blk, blk)
        hs = range(heads)
        cs = [slice(hh * dh, (hh + 1) * dh) for hh in hs]
        z = [lax.dot_general(q_ref[:, cs[hh]], k_ref[pl.ds(k0, blk), cs[hh]],
                             (((1,), (1,)), ((), ())), preferred_element_type=F32) for hh in hs]
        sp = []
        for hh in hs:
            neg_abs = lax.bitcast_convert_type(
                lax.bitcast_convert_type(z[hh], jnp.uint32) | sign, F32)
            s = jnp.maximum(z[hh], 0.0) + jnp.log(1.0 + jnp.exp2(neg_abs)) * LOG2E
            sp.append(jnp.where(strict_lower, s, 0.0) if masked else s)
        tail = [jnp.dot(sp[hh].astype(BF16), tail_mat, preferred_element_type=F32) for hh in hs]
        w = []
        for hh in hs:
            e = jnp.exp2(z[hh] - sp[hh] - (tail[hh] + carries[hh]))
            w.append((jnp.where(strict_lower, e, 0.0) if masked else e).astype(BF16))
        for hh in hs:
            acc_ref[hh] += jnp.dot(w[hh], v_ref[pl.ds(k0, blk), cs[hh]],
                                   preferred_element_type=F32)
        return tuple(carries[hh] + jnp.sum(sp[hh], axis=1, keepdims=True) for hh in hs)

    acc_ref[...] = jnp.zeros_like(acc_ref)
    carries = tile(qi, tuple(jnp.zeros((blk, 1), F32) for _ in range(heads)), True)
    lax.fori_loop(0, qi, lambda i, c: tile(qi - 1 - i, c, False), carries)
    for hh in range(heads):
        o_ref[:, hh * dh:(hh + 1) * dh] = acc_ref[hh].astype(o_ref.dtype)


def sb_attention(qkv, batch, seq, n_heads, blk=256, heads=4):
    t = qkv.shape[0]
    dh = SB_HEAD_DIM
    nq = seq // blk
    hg = n_heads // heads
    kern = functools.partial(_sb_kernel, blk=blk, heads=heads)
    return pl.pallas_call(
        kern,
        grid=(batch, hg, nq),
        in_specs=[pl.BlockSpec((blk, heads * dh), lambda b, h, i: (b * nq + i, h)),
                  pl.BlockSpec((seq, heads * dh), lambda b, h, i: (b, hg + h)),
                  pl.BlockSpec((seq, heads * dh), lambda b, h, i: (b, 2 * hg + h))],
        out_specs=pl.BlockSpec((blk, heads * dh), lambda b, h, i: (b * nq + i, h)),
        out_shape=jax.ShapeDtypeStruct((t, n_heads * dh), BF16),
        scratch_shapes=[pltpu.VMEM((heads, blk, dh), F32)],
        compiler_params=_cparams("parallel", "parallel", "arbitrary"),
        name="sb_attention",
    )(qkv, qkv, qkv)


def _gather_rows_kernel(idx_ref, src_ref, o_ref, sem, *, rows):
    base = pl.program_id(0) * rows

    def row_copy(r, src_row):
        return pltpu.make_async_copy(src_ref.at[pl.ds(src_row, 1), :],
                                     o_ref.at[pl.ds(r, 1), :], sem)

    def issue(r, c):
        row_copy(r, idx_ref[base + r]).start()
        return c

    def drain(r, c):
        row_copy(r, 0).wait()
        return c

    lax.fori_loop(0, rows, issue, 0)
    lax.fori_loop(0, rows, drain, 0)


def gather_rows(src, idx, rows=256):
    n = idx.shape[0]
    d = src.shape[1]
    return pl.pallas_call(
        functools.partial(_gather_rows_kernel, rows=rows),
        grid_spec=pltpu.PrefetchScalarGridSpec(
            num_scalar_prefetch=1,
            grid=(n // rows,),
            in_specs=[pl.BlockSpec(memory_space=pl.ANY)],
            out_specs=pl.BlockSpec((rows, d), lambda i, idx: (i, 0)),
            scratch_shapes=[pltpu.SemaphoreType.DMA(())]),
        out_shape=jax.ShapeDtypeStruct((n, d), src.dtype),
        compiler_params=_cparams("arbitrary"),
        name="gather_rows",
    )(idx, src)


def _expert_changed(be_ref, i):
    return (i == 0) | (be_ref[i] != be_ref[jnp.maximum(i - 1, 0)])


def _moe_up_kernel(be_ref, nb_ref, x_ref, wg_ref, wu_ref, o_ref, wgs_ref, wus_ref):
    i = pl.program_id(1)

    @pl.when(i < nb_ref[0])
    def _():
        @pl.when(_expert_changed(be_ref, i))
        def _():
            wgs_ref[...] = wg_ref[...].astype(BF16)
            wus_ref[...] = wu_ref[...].astype(BF16)

        a = x_ref[...].astype(BF16)
        g = jnp.dot(a, wgs_ref[...], preferred_element_type=F32)
        u = jnp.dot(a, wus_ref[...], preferred_element_type=F32)
        o_ref[...] = (_silu(g) * u).astype(o_ref.dtype)

    @pl.when(i >= nb_ref[0])
    def _():
        o_ref[...] = jnp.zeros_like(o_ref)


def moe_up(xs, w_gate, w_up, layer, block_expert, n_used, tm, tn):
    pr, k = xs.shape
    f = w_gate.shape[3]
    w_spec = pl.BlockSpec((None, None, k, tn), lambda j, i, be, nb: (layer, be[i], 0, j))
    return pl.pallas_call(
        _moe_up_kernel,
        grid_spec=pltpu.PrefetchScalarGridSpec(
            num_scalar_prefetch=2,
            grid=(f // tn, pr // tm),
            in_specs=[pl.BlockSpec((tm, k), lambda j, i, be, nb: (i, 0)), w_spec, w_spec],
            out_specs=pl.BlockSpec((tm, tn), lambda j, i, be, nb: (i, j)),
            scratch_shapes=[pltpu.VMEM((k, tn), BF16), pltpu.VMEM((k, tn), BF16)]),
        out_shape=jax.ShapeDtypeStruct((pr, f), BF16),
        compiler_params=_cparams("arbitrary", "arbitrary"),
        name="moe_up",
    )(block_expert, n_used, xs, w_gate, w_up)


def _moe_down_kernel(be_ref, nb_ref, h_ref, w_ref, o_ref, ws_ref):
    i = pl.program_id(1)

    @pl.when(i < nb_ref[0])
    def _():
        @pl.when(_expert_changed(be_ref, i))
        def _():
            ws_ref[...] = w_ref[...].astype(BF16)

        o_ref[...] = jnp.dot(h_ref[...], ws_ref[...], preferred_element_type=F32)

    @pl.when(i >= nb_ref[0])
    def _():
        o_ref[...] = jnp.zeros_like(o_ref)


def moe_down(hid, w_down, layer, block_expert, n_used, tm, tn):
    pr, f = hid.shape
    d = w_down.shape[3]
    return pl.pallas_call(
        _moe_down_kernel,
        grid_spec=pltpu.PrefetchScalarGridSpec(
            num_scalar_prefetch=2,
            grid=(d // tn, pr // tm),
            in_specs=[pl.BlockSpec((tm, f), lambda j, i, be, nb: (i, 0)),
                      pl.BlockSpec((None, None, f, tn),
                                   lambda j, i, be, nb: (layer, be[i], 0, j))],
            out_specs=pl.BlockSpec((tm, tn), lambda j, i, be, nb: (i, j)),
            scratch_shapes=[pltpu.VMEM((f, tn), BF16)]),
        out_shape=jax.ShapeDtypeStruct((pr, d), F32),
        compiler_params=_cparams("arbitrary", "arbitrary"),
        name="moe_down",
    )(block_expert, n_used, hid, w_down)


def _moe_combine_kernel(pos_ref, y_ref, h_ref, g_ref, o_ref, buf_ref, sem, *, rows):
    base = pl.program_id(0) * rows

    def row_copy(r, src_row):
        return pltpu.make_async_copy(y_ref.at[pl.ds(src_row, 1), :],
                                     buf_ref.at[pl.ds(r, 1), :], sem)

    def issue(r, c):
        for k in range(TOP_K):
            row_copy(k * rows + r, pos_ref[TOP_K * (base + r) + k]).start()
        return c

    def drain(r, c):
        row_copy(r, 0).wait()
        return c

    lax.fori_loop(0, rows, issue, 0)
    lax.fori_loop(0, TOP_K * rows, drain, 0)
    g = g_ref[...]
    acc = h_ref[...]
    for k in range(TOP_K):
        acc = acc + g[:, k:k + 1] * buf_ref[pl.ds(k * rows, rows), :]
    o_ref[...] = acc


def moe_combine(y, pos, h, gates, rows=128):
    t, d = h.shape
    return pl.pallas_call(
        functools.partial(_moe_combine_kernel, rows=rows),
        grid_spec=pltpu.PrefetchScalarGridSpec(
            num_scalar_prefetch=1,
            grid=(t // rows,),
            in_specs=[pl.BlockSpec(memory_space=pl.ANY),
                      pl.BlockSpec((rows, d), lambda i, pos: (i, 0)),
                      pl.BlockSpec((rows, TOP_K), lambda i, pos: (i, 0))],
            out_specs=pl.BlockSpec((rows, d), lambda i, pos: (i, 0)),
            scratch_shapes=[pltpu.VMEM((TOP_K * rows, d), F32),
                            pltpu.SemaphoreType.DMA(())]),
        out_shape=jax.ShapeDtypeStruct((t, d), F32),
        compiler_params=_cparams("arbitrary"),
        name="moe_combine",
    )(pos, y, h, gates)


def moe_routing(idx, tm):
    t = idx.shape[0]
    n_pairs = t * TOP_K
    nb = n_pairs // tm + N_EXPERTS
    e_flat = idx.reshape(n_pairs)
    onehot = (e_flat[:, None] == jnp.arange(N_EXPERTS)[None, :]).astype(jnp.int32)
    ranks = jnp.cumsum(onehot, axis=0) - onehot
    counts = jnp.sum(onehot, axis=0)
    blocks = (counts + tm - 1) // tm
    blk_end = jnp.cumsum(blocks)
    blk_start = blk_end - blocks
    rank = jnp.sum(ranks * onehot, axis=1)
    pos = blk_start[e_flat] * tm + rank
    src_token = jnp.zeros((nb * tm,), jnp.int32).at[pos].set(
        jnp.arange(n_pairs, dtype=jnp.int32) // TOP_K)
    n_used = blk_end[-1]
    bid = jnp.arange(nb)
    be = jnp.sum((bid[:, None] >= blk_end[None, :]).astype(jnp.int32), axis=1)
    last_e = jnp.sum((n_used - 1 >= blk_end).astype(jnp.int32))
    block_expert = jnp.where(bid < n_used, be, last_e).astype(jnp.int32)
    return src_token, pos.astype(jnp.int32), block_expert, n_used.reshape(1).astype(jnp.int32)


def moe_layer(h, norm_w, w_router, w_gate, w_up, w_down, layer, tm=256):
    u, gates, idx = rmsnorm_router(h, norm_w, w_router)
    src_token, pos, block_expert, n_used = moe_routing(idx, tm)
    xs = gather_rows(u, src_token)
    hid = moe_up(xs, w_gate, w_up, layer, block_expert, n_used, tm, 512)
    y = moe_down(hid, w_down, layer, block_expert, n_used, tm, 512)
    return moe_combine(y, pos, h, gates)


def ssd_layer(h, norm_w, w_in, conv_w, conv_b, dt_bias, a_log, d_skip, ssd_norm_w, w_out,
              layer, batch, seq):
    d_inner = ssd_norm_w.shape[0]
    u = rmsnorm(h, norm_w, BF16)
    gz_t, xbc_t, dt_t = ssd_in_projection(u, w_in, layer, conv_w, conv_b, dt_bias, d_inner, seq,
                                          tm=min(1024, seq // 2))
    g_t = ssd_core(gz_t, xbc_t, dt_t, a_log, d_skip, ssd_norm_w, batch, seq)
    return matmul(g_t, w_out, layer, F32, 1024, 512, residual=h, transposed_a=True)


def sb_layer(h, norm_w, w_qkv, w_out, layer, batch, seq):
    d = h.shape[1]
    u = rmsnorm(h, norm_w, BF16)
    qkv = matmul(u, w_qkv, layer, BF16, 1024, 512, scaled_cols=d,
                 scale=LOG2E / math.sqrt(SB_HEAD_DIM))
    o = sb_attention(qkv, batch, seq, d // SB_HEAD_DIM)
    return matmul(o, w_out, layer, F32, 1024, 512, residual=h)


def swiglu_layer(h, norm_w, w_gate, w_up, w_down, layer):
    u = rmsnorm(h, norm_w, BF16)
    hid = matmul_swiglu(u, w_gate, w_up, layer, 1024, 512)
    return matmul(hid, w_down, layer, F32, 512, 512, residual=h)


def kernel(x, norm_mix_w, norm_ffn_w, norm_final_w, ssd_w_in, ssd_conv_w, ssd_conv_b, ssd_dt_bias, ssd_A_log, ssd_D, ssd_norm_w, ssd_w_out, sb_w_qkv, sb_w_out, ffn_w_gate, ffn_w_up, ffn_w_down, moe_w_router, moe_w_gate, moe_w_up, moe_w_down):
    batch, seq, d = x.shape
    depth = norm_mix_w.shape[0]
    h = x.reshape(batch * seq, d)
    for i in range(depth):
        j = i // 2
        if i % 2 == 0:
            h = ssd_layer(h, norm_mix_w[i], ssd_w_in, ssd_conv_w[j], ssd_conv_b[j],
                          ssd_dt_bias[j], ssd_A_log[j], ssd_D[j], ssd_norm_w[j], ssd_w_out,
                          j, batch, seq)
            h = swiglu_layer(h, norm_ffn_w[i], ffn_w_gate, ffn_w_up, ffn_w_down, j)
        else:
            h = sb_layer(h, norm_mix_w[i], sb_w_qkv, sb_w_out, j, batch, seq)
            h = moe_layer(h, norm_ffn_w[i], moe_w_router[j], moe_w_gate, moe_w_up,
                          moe_w_down, j)
    return rmsnorm(h, norm_final_w, F32).reshape(batch, seq, d)
```

```python
import functools
import math

import jax
import jax.numpy as jnp
from jax import lax
from jax.experimental import pallas as pl
from jax.experimental.pallas import tpu as pltpu

F32 = jnp.float32
BF16 = jnp.bfloat16

SSD_HEAD_DIM = 64
SSD_GROUPS = 8
SSD_STATE = 128
SSD_CONV = 4
GATED_NORM_EPS = 1e-5
SB_HEAD_DIM = 128
N_EXPERTS = 8
TOP_K = 2
RMS_EPS = 1e-6

LANES = 128
SUBLANES = 8
VMEM_LIMIT_BYTES = 56 * 1024 * 1024

SSD_Q = 128
LOG2E = 1.4426950408889634
LN2 = 0.6931471805599453


def _cparams(*sem):
    return pltpu.CompilerParams(dimension_semantics=sem,
                                vmem_limit_bytes=VMEM_LIMIT_BYTES)


def _silu(x):
    return x * (1.0 / (1.0 + jnp.exp(-x)))


def _softplus(x):
    return jnp.maximum(x, 0.0) + jnp.log1p(jnp.exp(-jnp.abs(x)))


def _rmsnorm_rows(x, w):
    ms = jnp.mean(x * x, axis=-1, keepdims=True)
    return x * lax.rsqrt(ms + RMS_EPS) * w


def _rmsnorm_kernel(h_ref, w_ref, o_ref):
    o_ref[...] = _rmsnorm_rows(h_ref[...], w_ref[...]).astype(o_ref.dtype)


def rmsnorm(h, w, out_dtype, tm=512):
    t, d = h.shape
    return pl.pallas_call(
        _rmsnorm_kernel,
        grid=(t // tm,),
        in_specs=[pl.BlockSpec((tm, d), lambda i: (i, 0)),
                  pl.BlockSpec((1, d), lambda i: (0, 0))],
        out_specs=pl.BlockSpec((tm, d), lambda i: (i, 0)),
        out_shape=jax.ShapeDtypeStruct((t, d), out_dtype),
        compiler_params=_cparams("parallel"),
        name="rmsnorm",
    )(h, w.reshape(1, d))


def _split3(x):
    hi = x.astype(BF16)
    r1 = x - hi.astype(F32)
    mid = r1.astype(BF16)
    lo = (r1 - mid.astype(F32)).astype(BF16)
    return hi, mid, lo


def _dot_f32(a, b):
    a0, a1, a2 = _split3(a)
    b0, b1, b2 = _split3(b)
    d = functools.partial(jnp.dot, preferred_element_type=F32)
    return (d(a0, b0) + (d(a0, b1) + d(a1, b0))
            + (d(a0, b2) + d(a1, b1) + d(a2, b0)))


def _rmsnorm_router_kernel(h_ref, w_ref, wr_ref, u_ref, gate_ref, idx_ref):
    u = _rmsnorm_rows(h_ref[...], w_ref[...])
    tm, d = u.shape
    s = d // LANES
    for c in range(s):
        u_ref[pl.ds(c, tm, stride=s), :] = u[:, c * LANES:(c + 1) * LANES]
    logits = _dot_f32(u, wr_ref[...])
    lane = lax.broadcasted_iota(jnp.int32, logits.shape, 1)
    lane_f = lane.astype(F32)
    neg = jnp.float32(-jnp.inf)
    logits = jnp.where(lane < N_EXPERTS, logits, neg)
    m1 = jnp.max(logits, axis=-1, keepdims=True)
    i1 = jnp.min(jnp.where(logits == m1, lane_f, float(LANES)), axis=-1, keepdims=True)
    rest = jnp.where(lane_f == i1, neg, logits)
    m2 = jnp.max(rest, axis=-1, keepdims=True)
    i2 = jnp.min(jnp.where(rest == m2, lane_f, float(LANES)), axis=-1, keepdims=True)
    e2 = jnp.exp(m2 - m1)
    g1 = 1.0 / (1.0 + e2)
    g2 = e2 * g1
    gate_ref[...] = jnp.where(lane == 0, g1, jnp.where(lane == 1, g2, 0.0))
    idx_ref[...] = jnp.where(lane == 0, i1, jnp.where(lane == 1, i2, 0.0)).astype(jnp.int32)


def rmsnorm_router(h, w, w_router, tm=256):
    t, d = h.shape
    wr = jnp.zeros((d, LANES), F32).at[:, :N_EXPERTS].set(w_router)
    u, gates, idx = pl.pallas_call(
        _rmsnorm_router_kernel,
        grid=(t // tm,),
        in_specs=[pl.BlockSpec((tm, d), lambda i: (i, 0)),
                  pl.BlockSpec((1, d), lambda i: (0, 0)),
                  pl.BlockSpec((d, LANES), lambda i: (0, 0))],
        out_specs=[pl.BlockSpec((tm * (d // LANES), LANES), lambda i: (i, 0)),
                   pl.BlockSpec((tm, LANES), lambda i: (i, 0)),
                   pl.BlockSpec((tm, LANES), lambda i: (i, 0))],
        out_shape=[jax.ShapeDtypeStruct((t * (d // LANES), LANES), F32),
                   jax.ShapeDtypeStruct((t, LANES), F32),
                   jax.ShapeDtypeStruct((t, LANES), jnp.int32)],
        compiler_params=_cparams("parallel"),
        name="rmsnorm_router",
    )(h, w.reshape(1, d), wr)
    return u, gates[:, :TOP_K], idx[:, :TOP_K]


def _cast_weights_once(w_ref, ws_ref):
    @pl.when(pl.program_id(1) == 0)
    def _():
        ws_ref[...] = w_ref[...].astype(BF16)


def _mm_kernel(a_ref, w_ref, o_ref, ws_ref, *, scaled_tiles, scale):
    _cast_weights_once(w_ref, ws_ref)
    acc = jnp.dot(a_ref[...], ws_ref[...], preferred_element_type=F32)
    if scaled_tiles:
        acc = acc * jnp.where(pl.program_id(0) < scaled_tiles, scale, 1.0)
    o_ref[...] = acc.astype(o_ref.dtype)


def _mm_res_kernel(a_ref, w_ref, r_ref, o_ref, ws_ref):
    _cast_weights_once(w_ref, ws_ref)
    o_ref[...] = r_ref[...] + jnp.dot(a_ref[...], ws_ref[...], preferred_element_type=F32)


def _mm_tin_res_kernel(at_ref, w_ref, r_ref, o_ref, ws_ref):
    _cast_weights_once(w_ref, ws_ref)
    acc = lax.dot_general(at_ref[...], ws_ref[...], (((0,), (0,)), ((), ())),
                          preferred_element_type=F32)
    o_ref[...] = r_ref[...] + acc


def _mm_swiglu_kernel(a_ref, wg_ref, wu_ref, o_ref, wgs_ref, wus_ref):
    _cast_weights_once(wg_ref, wgs_ref)
    _cast_weights_once(wu_ref, wus_ref)
    a = a_ref[...]
    g = jnp.dot(a, wgs_ref[...], preferred_element_type=F32)
    u = jnp.dot(a, wus_ref[...], preferred_element_type=F32)
    o_ref[...] = (_silu(g) * u).astype(o_ref.dtype)


def matmul(a, w, layer, out_dtype, tm, tn, residual=None, transposed_a=False,
           scaled_cols=0, scale=1.0):
    if transposed_a:
        k, m = a.shape
        a_spec = pl.BlockSpec((k, tm), lambda j, i: (0, i))
    else:
        m, k = a.shape
        a_spec = pl.BlockSpec((tm, k), lambda j, i: (i, 0))
    n = w.shape[2]
    in_specs = [a_spec, pl.BlockSpec((None, k, tn), lambda j, i: (layer, 0, j))]
    args = [a, w]
    if residual is not None:
        in_specs.append(pl.BlockSpec((tm, tn), lambda j, i: (i, j)))
        args.append(residual)
        body = _mm_tin_res_kernel if transposed_a else _mm_res_kernel
    else:
        assert not transposed_a and scaled_cols % tn == 0
        body = functools.partial(_mm_kernel, scaled_tiles=scaled_cols // tn, scale=scale)
    return pl.pallas_call(
        body,
        grid=(n // tn, m // tm),
        in_specs=in_specs,
        out_specs=pl.BlockSpec((tm, tn), lambda j, i: (i, j)),
        out_shape=jax.ShapeDtypeStruct((m, n), out_dtype),
        scratch_shapes=[pltpu.VMEM((k, tn), BF16)],
        compiler_params=_cparams("arbitrary", "arbitrary"),
        name="matmul",
    )(*args)


def matmul_swiglu(a, w_gate, w_up, layer, tm, tn):
    m, k = a.shape
    n = w_gate.shape[2]
    w_spec = pl.BlockSpec((None, k, tn), lambda j, i: (layer, 0, j))
    return pl.pallas_call(
        _mm_swiglu_kernel,
        grid=(n // tn, m // tm),
        in_specs=[pl.BlockSpec((tm, k), lambda j, i: (i, 0)), w_spec, w_spec],
        out_specs=pl.BlockSpec((tm, tn), lambda j, i: (i, j)),
        out_shape=jax.ShapeDtypeStruct((m, n), BF16),
        scratch_shapes=[pltpu.VMEM((k, tn), BF16), pltpu.VMEM((k, tn), BF16)],
        compiler_params=_cparams("arbitrary", "arbitrary"),
        name="matmul_swiglu",
    )(a, w_gate, w_up)


INPROJ_ROWS = 256
_NT = (((1,), (1,)), ((), ()))


def _inproj_gate_kernel(a_ref, wt_ref, o_ref, ws_ref):
    _cast_weights_once(wt_ref, ws_ref)
    rc = INPROJ_ROWS
    for c in range(a_ref.shape[0] // rc):
        zt = lax.dot_general(ws_ref[...], a_ref[pl.ds(c * rc, rc), :], _NT,
                             preferred_element_type=F32)
        o_ref[:, pl.ds(c * rc, rc)] = _silu(zt).astype(o_ref.dtype)


def _inproj_conv_kernel(a_ref, wt_ref, cw_ref, o_ref, ws_ref, halo_ref, pad_ref, *,
                        blocks_per_batch):
    _cast_weights_once(wt_ref, ws_ref)
    tm = a_ref.shape[0]
    rc = INPROJ_ROWS
    first = (pl.program_id(1) % blocks_per_batch) == 0
    pad_ref[pl.ds(0, SUBLANES), :] = jnp.where(first, 0.0, halo_ref[...])
    cw = cw_ref[...]
    for c in range(tm // rc):
        acc = lax.dot_general(a_ref[pl.ds(c * rc, rc), :], ws_ref[...], _NT,
                              preferred_element_type=F32)
        pad_ref[pl.ds(SUBLANES + c * rc, rc), :] = acc
        y = cw[SSD_CONV:SSD_CONV + 1, :] + cw[SSD_CONV - 1:SSD_CONV, :] * acc
        for k in range(SSD_CONV - 1):
            lag = SSD_CONV - 1 - k
            y = y + cw[k:k + 1, :] * pad_ref[pl.ds(SUBLANES + c * rc - lag, rc), :]
        o_ref[:, pl.ds(c * rc, rc)] = _silu(y).T
    halo_ref[...] = pad_ref[pl.ds(tm, SUBLANES), :]


def _inproj_dt_kernel(a_ref, wt_ref, b_ref, o_ref, *, n_heads):
    raw = lax.dot_general(a_ref[...], wt_ref[...].astype(BF16), _NT,
                          preferred_element_type=F32)
    dt = _softplus(raw + b_ref[...])
    o_ref[...] = dt.T[:n_heads, :]


def ssd_in_projection(u, w_in_t, layer, conv_w, conv_b, dt_bias, d_inner, seq, tm=1024, tn=512):
    m, k = u.shape
    conv_dim = conv_w.shape[1]
    n_heads = dt_bias.shape[0]
    a_spec = pl.BlockSpec((tm, k), lambda j, i: (i, 0))
    cparams = _cparams("arbitrary", "arbitrary")

    gz_t = pl.pallas_call(
        _inproj_gate_kernel,
        grid=(d_inner // tn, m // tm),
        in_specs=[a_spec, pl.BlockSpec((None, tn, k), lambda j, i: (layer, j, 0))],
        out_specs=pl.BlockSpec((tn, tm), lambda j, i: (j, i)),
        out_shape=jax.ShapeDtypeStruct((d_inner, m), BF16),
        scratch_shapes=[pltpu.VMEM((tn, k), BF16)],
        compiler_params=cparams,
        name="ssd_inproj_gate",
    )(u, w_in_t)

    cw = jnp.zeros((SUBLANES, conv_dim), F32).at[:SSD_CONV].set(conv_w).at[SSD_CONV].set(conv_b)
    j0 = d_inner // tn
    xbc_t = pl.pallas_call(
        functools.partial(_inproj_conv_kernel, blocks_per_batch=seq // tm),
        grid=(conv_dim // tn, m // tm),
        in_specs=[a_spec,
                  pl.BlockSpec((None, tn, k), lambda j, i: (layer, j0 + j, 0)),
                  pl.BlockSpec((SUBLANES, tn), lambda j, i: (0, j))],
        out_specs=pl.BlockSpec((tn, tm), lambda j, i: (j, i)),
        out_shape=jax.ShapeDtypeStruct((conv_dim, m), F32),
        scratch_shapes=[pltpu.VMEM((tn, k), BF16),
                        pltpu.VMEM((SUBLANES, tn), F32),
                        pltpu.VMEM((tm + SUBLANES, tn), F32)],
        compiler_params=cparams,
        name="ssd_inproj_conv",
    )(u, w_in_t, cw)

    bias = jnp.zeros((1, LANES), F32).at[0, :n_heads].set(dt_bias)
    dt_t = pl.pallas_call(
        functools.partial(_inproj_dt_kernel, n_heads=n_heads),
        grid=(m // tm,),
        in_specs=[pl.BlockSpec((tm, k), lambda i: (i, 0)),
                  pl.BlockSpec((None, LANES, k),
                               lambda i: (layer, (d_inner + conv_dim) // LANES, 0)),
                  pl.BlockSpec((1, LANES), lambda i: (0, 0))],
        out_specs=pl.BlockSpec((n_heads, tm), lambda i: (0, i)),
        out_shape=jax.ShapeDtypeStruct((n_heads, m), F32),
        compiler_params=_cparams("parallel"),
        name="ssd_inproj_dt",
    )(u, w_in_t, bias)
    return gz_t, xbc_t, dt_t


def _ssd_kernel(gz_ref, x_ref, b_ref, c_ref, dt_ref, hp_ref, cp_ref,
                o_ref,
                yg_ref, state_ref, acs_ref, dts_ref, wgt_ref, od_ref, cdec_ref, *, tb, d_inner):
    q = SSD_Q
    p = SSD_HEAD_DIM
    n = SSD_STATE
    n_heads = d_inner // p
    heads_per_group = n_heads // SSD_GROUPS

    @pl.when(pl.program_id(1) == 0)
    def _():
        state_ref[...] = jnp.zeros_like(state_ref)

    hp = hp_ref[...]
    a_head = -jnp.exp(hp[:, 0:1])
    ri = lax.broadcasted_iota(jnp.int32, (q, q), 0)
    ci = lax.broadcasted_iota(jnp.int32, (q, q), 1)
    upper_incl = (ri <= ci)
    cum_mat = upper_incl.astype(BF16)

    for sub in range(tb // q):
        cols = slice(sub * q, (sub + 1) * q)
        dt = dt_ref[:, cols]
        a = dt * a_head
        a0, a1, a2 = _split3(a)
        acs = (jnp.dot(a0, cum_mat, preferred_element_type=F32)
               + jnp.dot(a1, cum_mat, preferred_element_type=F32)
               + jnp.dot(a2, cum_mat, preferred_element_type=F32))
        last = acs[:, q - 1:q]
        acs_ref[...] = acs
        dts_ref[...] = dt
        wgt_ref[...] = jnp.exp(last - acs) * dt
        od_ref[...] = jnp.exp(acs)
        cdec_ref[...] = jnp.exp(jnp.broadcast_to(last, (n_heads, n)))

        def group_body(g, carry):
            rg = pl.multiple_of(g * n, n)
            bt = b_ref[pl.ds(rg, n), cols]
            ct = c_ref[pl.ds(rg, n), cols]
            bg = bt.T.astype(BF16)
            cbt = jnp.dot(bg, ct.astype(BF16), preferred_element_type=F32)

            for e in range(heads_per_group):
                h = g * heads_per_group + e
                r0 = pl.multiple_of(h * p, p)
                acs_row = acs_ref[pl.ds(h, 1), :]
                row_b = jnp.broadcast_to(acs_row, (q, q))
                seg = row_b - row_b.T
                lt = jnp.exp(jnp.where(upper_incl, seg, -jnp.inf))
                mt = (cbt * lt).astype(BF16)
                xt = x_ref[pl.ds(r0, p), cols]
                xdt = (xt * dts_ref[pl.ds(h, 1), :]).astype(BF16)
                prev = state_ref[h]
                odct = (ct * od_ref[pl.ds(h, 1), :]).astype(BF16)
                yt = (jnp.dot(xdt, mt, preferred_element_type=F32)
                      + jnp.dot(prev.astype(BF16), odct, preferred_element_type=F32))
                dsk = hp_ref[pl.ds(h, 1), :][:, 1:2]
                yt = yt + xt * dsk
                yg_ref[pl.ds(r0, p), cols] = yt * gz_ref[pl.ds(r0, p), cols].astype(F32)
                xw = (xt * wgt_ref[pl.ds(h, 1), :]).astype(BF16)
                state_ref[h] = (prev * cdec_ref[pl.ds(h, 1), :]
                                + jnp.dot(xw, bg, preferred_element_type=F32))
            return carry

        lax.fori_loop(0, SSD_GROUPS, group_body, 0)

    gsz = d_inner // SSD_GROUPS

    def norm_body(g, carry):
        r0 = pl.multiple_of(g * gsz, gsz)
        y = yg_ref[pl.ds(r0, gsz), :]
        ms = jnp.mean(y * y, axis=0, keepdims=True)
        y = y * lax.rsqrt(ms + GATED_NORM_EPS) * cp_ref[pl.ds(r0, gsz), :]
        o_ref[pl.ds(r0, gsz), :] = y.astype(o_ref.dtype)
        return carry

    lax.fori_loop(0, SSD_GROUPS, norm_body, 0)


def ssd_core(gz_t, xbc_t, dt_t, a_log, d_skip, norm_w, batch, seq, tb=256):
    n_heads = a_log.shape[0]
    d_inner = n_heads * SSD_HEAD_DIM
    gn = SSD_GROUPS * SSD_STATE
    t = gz_t.shape[1]
    nblk = seq // tb
    hp = jnp.zeros((n_heads, 8), F32).at[:, 0].set(a_log).at[:, 1].set(d_skip)
    cp = norm_w.reshape(d_inner, 1)

    def tmap(row_block):
        return lambda b, i: (row_block, b * nblk + i)

    const = lambda b, i: (0, 0)
    kern = functools.partial(_ssd_kernel, tb=tb, d_inner=d_inner)
    return pl.pallas_call(
        kern,
        grid=(batch, nblk),
        in_specs=[pl.BlockSpec((d_inner, tb), tmap(0)),
                  pl.BlockSpec((d_inner, tb), tmap(0)),
                  pl.BlockSpec((gn, tb), tmap(d_inner // gn)),
                  pl.BlockSpec((gn, tb), tmap(d_inner // gn + 1)),
                  pl.BlockSpec((n_heads, tb), tmap(0)),
                  pl.BlockSpec((n_heads, 8), const),
                  pl.BlockSpec((d_inner, 1), const)],
        out_specs=pl.BlockSpec((d_inner, tb), tmap(0)),
        out_shape=jax.ShapeDtypeStruct((d_inner, t), BF16),
        scratch_shapes=[pltpu.VMEM((d_inner, tb), F32),
                        pltpu.VMEM((n_heads, SSD_HEAD_DIM, SSD_STATE), F32),
                        pltpu.VMEM((n_heads, SSD_Q), F32),
                        pltpu.VMEM((n_heads, SSD_Q), F32),
                        pltpu.VMEM((n_heads, SSD_Q), F32),
                        pltpu.VMEM((n_heads, SSD_Q), F32),
                        pltpu.VMEM((n_heads, SSD_STATE), F32)],
        compiler_params=_cparams("arbitrary", "arbitrary"),
        name="ssd_core",
    )(gz_t, xbc_t, xbc_t, xbc_t, dt_t, hp, cp)


def _sb_kernel(q_ref, k_ref, v_ref, o_ref, acc_ref, *, blk, heads):
    qi = pl.program_id(2)
    dh = SB_HEAD_DIM
    ri = lax.broadcasted_iota(jnp.int32, (blk, blk), 0)
    ci = lax.broadcasted_iota(jnp.int32, (blk, blk), 1)
    strict_lower = (ri > ci)
    tail_mat = strict_lower.astype(BF16)
    sign = jnp.uint32(0x80000000)

    def tile(kb, carries, masked):
        k0 = pl.multiple_of(kb * blk, blk)
        hs = range(heads)
        cs = [slice(hh * dh, (hh + 1) * dh) for hh in hs]
        z = [lax.dot_general(q_ref[:, cs[hh]], k_ref[pl.ds(k0, blk), cs[hh]],
                             (((1,), (1,)), ((), ())), preferred_element_type=F32) for hh in hs]
        sp = []
        for hh in hs:
            neg_abs = lax.bitcast_convert_type(
                lax.bitcast_convert_type(z[hh], jnp.uint32) | sign, F32)
            s = jnp.maximum(z[hh], 0.0) + jnp.log(1.0 + jnp.exp2(neg_abs)) * LOG2E
            sp.append(jnp.where(strict_lower, s, 0.0) if masked else s)
        tail = [jnp.dot(sp[hh].astype(BF16), tail_mat, preferred_element_type=F32) for hh in hs]
        w = []
        for hh in hs:
            e = jnp.exp2(z[hh] - sp[hh] - (tail[hh] + carries[hh]))
            w.append((jnp.where(strict_lower, e, 0.0) if masked else e).astype(BF16))
        for hh in hs:
            acc_ref[hh] += jnp.dot(w[hh], v_ref[pl.ds(k0, blk), cs[hh]],
                                   preferred_element_type=F32)
        return tuple(carries[hh] + jnp.sum(sp[hh], axis=1, keepdims=True) for hh in hs)

    acc_ref[...] = jnp.zeros_like(acc_ref)
    carries = tile(qi, tuple(jnp.zeros((blk, 1), F32) for _ in range(heads)), True)
    lax.fori_loop(0, qi, lambda i, c: tile(qi - 1 - i, c, False), carries)
    for hh in range(heads):
        o_ref[:, hh * dh:(hh + 1) * dh] = acc_ref[hh].astype(o_ref.dtype)


def sb_attention(qkv, batch, seq, n_heads, blk=256, heads=4):
    t = qkv.shape[0]
    dh = SB_HEAD_DIM
    nq = seq // blk
    hg = n_heads // heads
    kern = functools.partial(_sb_kernel, blk=blk, heads=heads)
    return pl.pallas_call(
        kern,
        grid=(batch, hg, nq),
        in_specs=[pl.BlockSpec((blk, heads * dh), lambda b, h, i: (b * nq + i, h)),
                  pl.BlockSpec((seq, heads * dh), lambda b, h, i: (b, hg + h)),
                  pl.BlockSpec((seq, heads * dh), lambda b, h, i: (b, 2 * hg + h))],
        out_specs=pl.BlockSpec((blk, heads * dh), lambda b, h, i: (b * nq + i, h)),
        out_shape=jax.ShapeDtypeStruct((t, n_heads * dh), BF16),
        scratch_shapes=[pltpu.VMEM((heads, blk, dh), F32)],
        compiler_params=_cparams("parallel", "parallel", "arbitrary"),
        name="sb_attention",
    )(qkv, qkv, qkv)


def _gather_rows_kernel(idx_ref, src_ref, o_ref, buf_ref, sem, *, rows, slab):
    base = pl.program_id(0) * rows

    def row_copy(r, src_row):
        return pltpu.make_async_copy(
            src_ref.at[pl.ds(pl.multiple_of(src_row * slab, slab), slab), :],
            buf_ref.at[pl.ds(pl.multiple_of(r * slab, slab), slab), :], sem)

    def issue(r, c):
        row_copy(r, idx_ref[base + r]).start()
        return c

    def drain(r, c):
        row_copy(r, 0).wait()
        return c

    lax.fori_loop(0, rows, issue, 0)
    lax.fori_loop(0, rows, drain, 0)
    for c in range(slab):
        o_ref[:, c * LANES:(c + 1) * LANES] = (
            buf_ref[pl.ds(c, rows, stride=slab), :].astype(o_ref.dtype))


def gather_rows(src, idx, d, rows=256):
    n = idx.shape[0]
    slab = d // LANES
    return pl.pallas_call(
        functools.partial(_gather_rows_kernel, rows=rows, slab=slab),
        grid_spec=pltpu.PrefetchScalarGridSpec(
            num_scalar_prefetch=1,
            grid=(n // rows,),
            in_specs=[pl.BlockSpec(memory_space=pl.ANY)],
            out_specs=pl.BlockSpec((rows, d), lambda i, idx: (i, 0)),
            scratch_shapes=[pltpu.VMEM((rows * slab, LANES), src.dtype),
                            pltpu.SemaphoreType.DMA(())]),
        out_shape=jax.ShapeDtypeStruct((n, d), BF16),
        compiler_params=_cparams("arbitrary"),
        name="gather_rows",
    )(idx, src)


def _expert_changed(be_ref, i):
    return (i == 0) | (be_ref[i] != be_ref[jnp.maximum(i - 1, 0)])


def _moe_up_kernel(be_ref, nb_ref, x_ref, wg_ref, wu_ref, o_ref, wgs_ref, wus_ref):
    i = pl.program_id(1)

    @pl.when(i < nb_ref[0])
    def _():
        @pl.when(_expert_changed(be_ref, i))
        def _():
            wgs_ref[...] = wg_ref[...].astype(BF16)
            wus_ref[...] = wu_ref[...].astype(BF16)

        a = x_ref[...]
        g = jnp.dot(a, wgs_ref[...], preferred_element_type=F32)
        u = jnp.dot(a, wus_ref[...], preferred_element_type=F32)
        o_ref[...] = (_silu(g) * u).astype(o_ref.dtype)

    @pl.when(i >= nb_ref[0])
    def _():
        o_ref[...] = jnp.zeros_like(o_ref)


def moe_up(xs, w_gate, w_up, layer, block_expert, n_used, tm, tn):
    pr, k = xs.shape
    f = w_gate.shape[3]
    w_spec = pl.BlockSpec((None, None, k, tn), lambda j, i, be, nb: (layer, be[i], 0, j))
    return pl.pallas_call(
        _moe_up_kernel,
        grid_spec=pltpu.PrefetchScalarGridSpec(
            num_scalar_prefetch=2,
            grid=(f // tn, pr // tm),
            in_specs=[pl.BlockSpec((tm, k), lambda j, i, be, nb: (i, 0)), w_spec, w_spec],
            out_specs=pl.BlockSpec((tm, tn), lambda j, i, be, nb: (i, j)),
            scratch_shapes=[pltpu.VMEM((k, tn), BF16), pltpu.VMEM((k, tn), BF16)]),
        out_shape=jax.ShapeDtypeStruct((pr, f), BF16),
        compiler_params=_cparams("arbitrary", "arbitrary"),
        name="moe_up",
    )(block_expert, n_used, xs, w_gate, w_up)


def _moe_down_kernel(be_ref, nb_ref, h_ref, w_ref, o_ref, ws_ref):
    i = pl.program_id(1)

    @pl.when(i < nb_ref[0])
    def _():
        @pl.when(_expert_changed(be_ref, i))
        def _():
            ws_ref[...] = w_ref[...].astype(BF16)

        o_ref[...] = jnp.dot(h_ref[...], ws_ref[...], preferred_element_type=F32)

    @pl.when(i >= nb_ref[0])
    def _():
        o_ref[...] = jnp.zeros_like(o_ref)


def moe_down(hid, w_down, layer, block_expert, n_used, tm, tn):
    pr, f = hid.shape
    d = w_down.shape[3]
    return pl.pallas_call(
        _moe_down_kernel,
        grid_spec=pltpu.PrefetchScalarGridSpec(
            num_scalar_prefetch=2,
            grid=(d // tn, pr // tm),
            in_specs=[pl.BlockSpec((tm, f), lambda j, i, be, nb: (i, 0)),
                      pl.BlockSpec((None, None, f, tn),
                                   lambda j, i, be, nb: (layer, be[i], 0, j))],
            out_specs=pl.BlockSpec((tm, tn), lambda j, i, be, nb: (i, j)),
            scratch_shapes=[pltpu.VMEM((f, tn), BF16)]),
        out_shape=jax.ShapeDtypeStruct((pr, d), F32),
        compiler_params=_cparams("arbitrary", "arbitrary"),
        name="moe_down",
    )(block_expert, n_used, hid, w_down)


def _moe_combine_kernel(pos_ref, y_ref, h_ref, g_ref, o_ref, buf_ref, sem, *, rows):
    base = pl.program_id(0) * rows

    def row_copy(r, src_row):
        return pltpu.make_async_copy(y_ref.at[pl.ds(src_row, 1), :],
                                     buf_ref.at[pl.ds(r, 1), :], sem)

    def issue(r, c):
        for k in range(TOP_K):
            row_copy(k * rows + r, pos_ref[TOP_K * (base + r) + k]).start()
        return c

    def drain(r, c):
        row_copy(r, 0).wait()
        return c

    lax.fori_loop(0, rows, issue, 0)
    lax.fori_loop(0, TOP_K * rows, drain, 0)
    g = g_ref[...]
    acc = h_ref[...]
    for k in range(TOP_K):
        acc = acc + g[:, k:k + 1] * buf_ref[pl.ds(k * rows, rows), :]
    o_ref[...] = acc


def moe_combine(y, pos, h, gates, rows=128):
    t, d = h.shape
    return pl.pallas_call(
        functools.partial(_moe_combine_kernel, rows=rows),
        grid_spec=pltpu.PrefetchScalarGridSpec(
            num_scalar_prefetch=1,
            grid=(t // rows,),
            in_specs=[pl.BlockSpec(memory_space=pl.ANY),
                      pl.BlockSpec((rows, d), lambda i, pos: (i, 0)),
                      pl.BlockSpec((rows, TOP_K), lambda i, pos: (i, 0))],
            out_specs=pl.BlockSpec((rows, d), lambda i, pos: (i, 0)),
            scratch_shapes=[pltpu.VMEM((TOP_K * rows, d), F32),
                            pltpu.SemaphoreType.DMA(())]),
        out_shape=jax.ShapeDtypeStruct((t, d), F32),
        compiler_params=_cparams("arbitrary"),
        name="moe_combine",
    )(pos, y, h, gates)


def moe_routing(idx, tm):
    t = idx.shape[0]
    n_pairs = t * TOP_K
    nb = n_pairs // tm + N_EXPERTS
    e_flat = idx.reshape(n_pairs)
    onehot = (e_flat[:, None] == jnp.arange(N_EXPERTS)[None, :]).astype(jnp.int32)
    ranks = jnp.cumsum(onehot, axis=0) - onehot
    counts = jnp.sum(onehot, axis=0)
    blocks = (counts + tm - 1) // tm
    blk_end = jnp.cumsum(blocks)
    blk_start = blk_end - blocks
    rank = jnp.sum(ranks * onehot, axis=1)
    pos = blk_start[e_flat] * tm + rank
    src_token = jnp.zeros((nb * tm,), jnp.int32).at[pos].set(
        jnp.arange(n_pairs, dtype=jnp.int32) // TOP_K)
    n_used = blk_end[-1]
    bid = jnp.arange(nb)
    be = jnp.sum((bid[:, None] >= blk_end[None, :]).astype(jnp.int32), axis=1)
    last_e = jnp.sum((n_used - 1 >= blk_end).astype(jnp.int32))
    block_expert = jnp.where(bid < n_used, be, last_e).astype(jnp.int32)
    return src_token, pos.astype(jnp.int32), block_expert, n_used.reshape(1).astype(jnp.int32)


def moe_layer(h, norm_w, w_router, w_gate, w_up, w_down, layer, tm=512):
    u, gates, idx = rmsnorm_router(h, norm_w, w_router)
    src_token, pos, block_expert, n_used = moe_routing(idx, tm)
    xs = gather_rows(u, src_token, h.shape[1])
    hid = moe_up(xs, w_gate, w_up, layer, block_expert, n_used, tm, 512)
    y = moe_down(hid, w_down, layer, block_expert, n_used, tm, 512)
    return moe_combine(y, pos, h, gates)


def ssd_layer(h, norm_w, w_in_t, conv_w, conv_b, dt_bias, a_log, d_skip, ssd_norm_w, w_out,
              layer, batch, seq):
    d_inner = ssd_norm_w.shape[0]
    u = rmsnorm(h, norm_w, BF16)
    gz_t, xbc_t, dt_t = ssd_in_projection(u, w_in_t, layer, conv_w, conv_b, dt_bias, d_inner, seq,
                                          tm=min(1024, seq // 2))
    g_t = ssd_core(gz_t, xbc_t, dt_t, a_log, d_skip, ssd_norm_w, batch, seq)
    return matmul(g_t, w_out, layer, F32, 1024, 512, residual=h, transposed_a=True)


def sb_layer(h, norm_w, w_qkv, w_out, layer, batch, seq):
    d = h.shape[1]
    u = rmsnorm(h, norm_w, BF16)
    qkv = matmul(u, w_qkv, layer, BF16, 1024, 512, scaled_cols=d,
                 scale=LOG2E / math.sqrt(SB_HEAD_DIM))
    o = sb_attention(qkv, batch, seq, d // SB_HEAD_DIM)
    return matmul(o, w_out, layer, F32, 1024, 512, residual=h)


def swiglu_layer(h, norm_w, w_gate, w_up, w_down, layer):
    u = rmsnorm(h, norm_w, BF16)
    hid = matmul_swiglu(u, w_gate, w_up, layer, 1024, 512)
    return matmul(hid, w_down, layer, F32, 512, 512, residual=h)


def kernel(x, norm_mix_w, norm_ffn_w, norm_final_w, ssd_w_in, ssd_conv_w, ssd_conv_b, ssd_dt_bias, ssd_A_log, ssd_D, ssd_norm_w, ssd_w_out, sb_w_qkv, sb_w_out, ffn_w_gate, ffn_w_up, ffn_w_down, moe_w_router, moe_w_gate, moe_w_up, moe_w_down):
    batch, seq, d = x.shape
    depth = norm_mix_w.shape[0]
    h = x.reshape(batch * seq, d)
    ssd_w_in_t = jnp.swapaxes(ssd_w_in, 1, 2)
    for i in range(depth):
        j = i // 2
        if i % 2 == 0:
            h = ssd_layer(h, norm_mix_w[i], ssd_w_in_t, ssd_conv_w[j], ssd_conv_b[j],
                          ssd_dt_bias[j], ssd_A_log[j], ssd_D[j], ssd_norm_w[j], ssd_w_out,
                          j, batch, seq)
            h = swiglu_layer(h, norm_ffn_w[i], ffn_w_gate, ffn_w_up, ffn_w_down, j)
        else:
            h = sb_layer(h, norm_mix_w[i], sb_w_qkv, sb_w_out, j, batch, seq)
            h = moe_layer(h, norm_ffn_w[i], moe_w_router[j], moe_w_gate, moe_w_up,
                          moe_w_down, j)
    return rmsnorm(h, norm_final_w, F32).reshape(batch, seq, d)
```

```python
import functools
import math

import jax
import jax.numpy as jnp
from jax import lax
from jax.experimental import pallas as pl
from jax.experimental.pallas import tpu as pltpu

F32 = jnp.float32
BF16 = jnp.bfloat16

SSD_HEAD_DIM = 64
SSD_GROUPS = 8
SSD_STATE = 128
SSD_CONV = 4
GATED_NORM_EPS = 1e-5
SB_HEAD_DIM = 128
N_EXPERTS = 8
TOP_K = 2
RMS_EPS = 1e-6

LANES = 128
SUBLANES = 8
VMEM_LIMIT_BYTES = 56 * 1024 * 1024

SSD_Q = 128
LOG2E = 1.4426950408889634
LN2 = 0.6931471805599453


def _cparams(*sem):
    return pltpu.CompilerParams(dimension_semantics=sem,
                                vmem_limit_bytes=VMEM_LIMIT_BYTES)


def _silu(x):
    return x * (1.0 / (1.0 + jnp.exp(-x)))


def _softplus(x):
    return jnp.maximum(x, 0.0) + jnp.log1p(jnp.exp(-jnp.abs(x)))


def _rmsnorm_rows(x, w):
    ms = jnp.mean(x * x, axis=-1, keepdims=True)
    return x * lax.rsqrt(ms + RMS_EPS) * w


def _rmsnorm_kernel(h_ref, w_ref, o_ref):
    o_ref[...] = _rmsnorm_rows(h_ref[...], w_ref[...]).astype(o_ref.dtype)


def rmsnorm(h, w, out_dtype, tm=512):
    t, d = h.shape
    return pl.pallas_call(
        _rmsnorm_kernel,
        grid=(t // tm,),
        in_specs=[pl.BlockSpec((tm, d), lambda i: (i, 0)),
                  pl.BlockSpec((1, d), lambda i: (0, 0))],
        out_specs=pl.BlockSpec((tm, d), lambda i: (i, 0)),
        out_shape=jax.ShapeDtypeStruct((t, d), out_dtype),
        compiler_params=_cparams("parallel"),
        name="rmsnorm",
    )(h, w.reshape(1, d))


def _split3(x):
    hi = x.astype(BF16)
    r1 = x - hi.astype(F32)
    mid = r1.astype(BF16)
    lo = (r1 - mid.astype(F32)).astype(BF16)
    return hi, mid, lo


def _dot_f32(a, b):
    a0, a1, a2 = _split3(a)
    b0, b1, b2 = _split3(b)
    d = functools.partial(jnp.dot, preferred_element_type=F32)
    return (d(a0, b0) + (d(a0, b1) + d(a1, b0))
            + (d(a0, b2) + d(a1, b1) + d(a2, b0)))


def _rmsnorm_router_kernel(h_ref, w_ref, wr_ref, u_ref, gate_ref, idx_ref):
    u = _rmsnorm_rows(h_ref[...], w_ref[...])
    u_ref[...] = u
    logits = _dot_f32(u, wr_ref[...])
    lane = lax.broadcasted_iota(jnp.int32, logits.shape, 1)
    lane_f = lane.astype(F32)
    neg = jnp.float32(-jnp.inf)
    logits = jnp.where(lane < N_EXPERTS, logits, neg)
    m1 = jnp.max(logits, axis=-1, keepdims=True)
    i1 = jnp.min(jnp.where(logits == m1, lane_f, float(LANES)), axis=-1, keepdims=True)
    rest = jnp.where(lane_f == i1, neg, logits)
    m2 = jnp.max(rest, axis=-1, keepdims=True)
    i2 = jnp.min(jnp.where(rest == m2, lane_f, float(LANES)), axis=-1, keepdims=True)
    e2 = jnp.exp(m2 - m1)
    g1 = 1.0 / (1.0 + e2)
    g2 = e2 * g1
    gate_ref[...] = jnp.where(lane == 0, g1, jnp.where(lane == 1, g2, 0.0))
    idx_ref[...] = jnp.where(lane == 0, i1, jnp.where(lane == 1, i2, 0.0)).astype(jnp.int32)


def rmsnorm_router(h, w, w_router, tm=256):
    t, d = h.shape
    wr = jnp.zeros((d, LANES), F32).at[:, :N_EXPERTS].set(w_router)
    u, gates, idx = pl.pallas_call(
        _rmsnorm_router_kernel,
        grid=(t // tm,),
        in_specs=[pl.BlockSpec((tm, d), lambda i: (i, 0)),
                  pl.BlockSpec((1, d), lambda i: (0, 0)),
                  pl.BlockSpec((d, LANES), lambda i: (0, 0))],
        out_specs=[pl.BlockSpec((tm, d), lambda i: (i, 0)),
                   pl.BlockSpec((tm, LANES), lambda i: (i, 0)),
                   pl.BlockSpec((tm, LANES), lambda i: (i, 0))],
        out_shape=[jax.ShapeDtypeStruct((t, d), F32),
                   jax.ShapeDtypeStruct((t, LANES), F32),
                   jax.ShapeDtypeStruct((t, LANES), jnp.int32)],
        compiler_params=_cparams("parallel"),
        name="rmsnorm_router",
    )(h, w.reshape(1, d), wr)
    return u, gates[:, :TOP_K], idx[:, :TOP_K]


def _cast_weights_once(w_ref, ws_ref):
    @pl.when(pl.program_id(1) == 0)
    def _():
        ws_ref[...] = w_ref[...].astype(BF16)


def _mm_kernel(a_ref, w_ref, o_ref, ws_ref, *, scaled_tiles, scale):
    _cast_weights_once(w_ref, ws_ref)
    acc = jnp.dot(a_ref[...], ws_ref[...], preferred_element_type=F32)
    if scaled_tiles:
        acc = acc * jnp.where(pl.program_id(0) < scaled_tiles, scale, 1.0)
    o_ref[...] = acc.astype(o_ref.dtype)


def _mm_res_kernel(a_ref, w_ref, r_ref, o_ref, ws_ref):
    _cast_weights_once(w_ref, ws_ref)
    o_ref[...] = r_ref[...] + jnp.dot(a_ref[...], ws_ref[...], preferred_element_type=F32)


def _mm_tin_res_kernel(at_ref, w_ref, r_ref, o_ref, ws_ref):
    _cast_weights_once(w_ref, ws_ref)
    acc = lax.dot_general(at_ref[...], ws_ref[...], (((0,), (0,)), ((), ())),
                          preferred_element_type=F32)
    o_ref[...] = r_ref[...] + acc


def _mm_swiglu_kernel(a_ref, wg_ref, wu_ref, o_ref, wgs_ref, wus_ref):
    _cast_weights_once(wg_ref, wgs_ref)
    _cast_weights_once(wu_ref, wus_ref)
    a = a_ref[...]
    g = jnp.dot(a, wgs_ref[...], preferred_element_type=F32)
    u = jnp.dot(a, wus_ref[...], preferred_element_type=F32)
    o_ref[...] = (_silu(g) * u).astype(o_ref.dtype)


def matmul(a, w, layer, out_dtype, tm, tn, residual=None, transposed_a=False,
           scaled_cols=0, scale=1.0):
    if transposed_a:
        k, m = a.shape
        a_spec = pl.BlockSpec((k, tm), lambda j, i: (0, i))
    else:
        m, k = a.shape
        a_spec = pl.BlockSpec((tm, k), lambda j, i: (i, 0))
    n = w.shape[2]
    in_specs = [a_spec, pl.BlockSpec((None, k, tn), lambda j, i: (layer, 0, j))]
    args = [a, w]
    if residual is not None:
        in_specs.append(pl.BlockSpec((tm, tn), lambda j, i: (i, j)))
        args.append(residual)
        body = _mm_tin_res_kernel if transposed_a else _mm_res_kernel
    else:
        assert not transposed_a and scaled_cols % tn == 0
        body = functools.partial(_mm_kernel, scaled_tiles=scaled_cols // tn, scale=scale)
    return pl.pallas_call(
        body,
        grid=(n // tn, m // tm),
        in_specs=in_specs,
        out_specs=pl.BlockSpec((tm, tn), lambda j, i: (i, j)),
        out_shape=jax.ShapeDtypeStruct((m, n), out_dtype),
        scratch_shapes=[pltpu.VMEM((k, tn), BF16)],
        compiler_params=_cparams("arbitrary", "arbitrary"),
        name="matmul",
    )(*args)


def matmul_swiglu(a, w_gate, w_up, layer, tm, tn):
    m, k = a.shape
    n = w_gate.shape[2]
    w_spec = pl.BlockSpec((None, k, tn), lambda j, i: (layer, 0, j))
    return pl.pallas_call(
        _mm_swiglu_kernel,
        grid=(n // tn, m // tm),
        in_specs=[pl.BlockSpec((tm, k), lambda j, i: (i, 0)), w_spec, w_spec],
        out_specs=pl.BlockSpec((tm, tn), lambda j, i: (i, j)),
        out_shape=jax.ShapeDtypeStruct((m, n), BF16),
        scratch_shapes=[pltpu.VMEM((k, tn), BF16), pltpu.VMEM((k, tn), BF16)],
        compiler_params=_cparams("arbitrary", "arbitrary"),
        name="matmul_swiglu",
    )(a, w_gate, w_up)


INPROJ_ROWS = 256
_NT = (((1,), (1,)), ((), ()))


def _inproj_gate_kernel(a_ref, wt_ref, o_ref, ws_ref):
    _cast_weights_once(wt_ref, ws_ref)
    rc = INPROJ_ROWS
    for c in range(a_ref.shape[0] // rc):
        zt = lax.dot_general(ws_ref[...], a_ref[pl.ds(c * rc, rc), :], _NT,
                             preferred_element_type=F32)
        o_ref[:, pl.ds(c * rc, rc)] = _silu(zt).astype(o_ref.dtype)


def _inproj_conv_kernel(a_ref, wt_ref, cw_ref, o_ref, ws_ref, halo_ref, pad_ref, *,
                        blocks_per_batch):
    _cast_weights_once(wt_ref, ws_ref)
    tm = a_ref.shape[0]
    rc = INPROJ_ROWS
    first = (pl.program_id(1) % blocks_per_batch) == 0
    pad_ref[pl.ds(0, SUBLANES), :] = jnp.where(first, 0.0, halo_ref[...])
    cw = cw_ref[...]
    for c in range(tm // rc):
        acc = lax.dot_general(a_ref[pl.ds(c * rc, rc), :], ws_ref[...], _NT,
                              preferred_element_type=F32)
        pad_ref[pl.ds(SUBLANES + c * rc, rc), :] = acc
        y = cw[SSD_CONV:SSD_CONV + 1, :] + cw[SSD_CONV - 1:SSD_CONV, :] * acc
        for k in range(SSD_CONV - 1):
            lag = SSD_CONV - 1 - k
            y = y + cw[k:k + 1, :] * pad_ref[pl.ds(SUBLANES + c * rc - lag, rc), :]
        o_ref[:, pl.ds(c * rc, rc)] = _silu(y).T
    halo_ref[...] = pad_ref[pl.ds(tm, SUBLANES), :]


def _inproj_dt_kernel(a_ref, wt_ref, b_ref, o_ref, *, n_heads):
    raw = lax.dot_general(a_ref[...], wt_ref[...].astype(BF16), _NT,
                          preferred_element_type=F32)
    dt = _softplus(raw + b_ref[...])
    o_ref[...] = dt.T[:n_heads, :]


def ssd_in_projection(u, w_in_t, layer, conv_w, conv_b, dt_bias, d_inner, seq, tm=1024, tn=512):
    m, k = u.shape
    conv_dim = conv_w.shape[1]
    n_heads = dt_bias.shape[0]
    a_spec = pl.BlockSpec((tm, k), lambda j, i: (i, 0))
    cparams = _cparams("arbitrary", "arbitrary")

    gz_t = pl.pallas_call(
        _inproj_gate_kernel,
        grid=(d_inner // tn, m // tm),
        in_specs=[a_spec, pl.BlockSpec((None, tn, k), lambda j, i: (layer, j, 0))],
        out_specs=pl.BlockSpec((tn, tm), lambda j, i: (j, i)),
        out_shape=jax.ShapeDtypeStruct((d_inner, m), BF16),
        scratch_shapes=[pltpu.VMEM((tn, k), BF16)],
        compiler_params=cparams,
        name="ssd_inproj_gate",
    )(u, w_in_t)

    cw = jnp.zeros((SUBLANES, conv_dim), F32).at[:SSD_CONV].set(conv_w).at[SSD_CONV].set(conv_b)
    j0 = d_inner // tn
    xbc_t = pl.pallas_call(
        functools.partial(_inproj_conv_kernel, blocks_per_batch=seq // tm),
        grid=(conv_dim // tn, m // tm),
        in_specs=[a_spec,
                  pl.BlockSpec((None, tn, k), lambda j, i: (layer, j0 + j, 0)),
                  pl.BlockSpec((SUBLANES, tn), lambda j, i: (0, j))],
        out_specs=pl.BlockSpec((tn, tm), lambda j, i: (j, i)),
        out_shape=jax.ShapeDtypeStruct((conv_dim, m), F32),
        scratch_shapes=[pltpu.VMEM((tn, k), BF16),
                        pltpu.VMEM((SUBLANES, tn), F32),
                        pltpu.VMEM((tm + SUBLANES, tn), F32)],
        compiler_params=cparams,
        name="ssd_inproj_conv",
    )(u, w_in_t, cw)

    bias = jnp.zeros((1, LANES), F32).at[0, :n_heads].set(dt_bias)
    dt_t = pl.pallas_call(
        functools.partial(_inproj_dt_kernel, n_heads=n_heads),
        grid=(m // tm,),
        in_specs=[pl.BlockSpec((tm, k), lambda i: (i, 0)),
                  pl.BlockSpec((None, LANES, k),
                               lambda i: (layer, (d_inner + conv_dim) // LANES, 0)),
                  pl.BlockSpec((1, LANES), lambda i: (0, 0))],
        out_specs=pl.BlockSpec((n_heads, tm), lambda i: (0, i)),
        out_shape=jax.ShapeDtypeStruct((n_heads, m), F32),
        compiler_params=_cparams("parallel"),
        name="ssd_inproj_dt",
    )(u, w_in_t, bias)
    return gz_t, xbc_t, dt_t


def _ssd_kernel(gz_ref, x_ref, b_ref, c_ref, dt_ref, hp_ref, cp_ref,
                o_ref,
                yg_ref, state_ref, acs_ref, dts_ref, wgt_ref, od_ref, cdec_ref, *, tb, d_inner):
    q = SSD_Q
    p = SSD_HEAD_DIM
    n = SSD_STATE
    n_heads = d_inner // p
    heads_per_group = n_heads // SSD_GROUPS

    @pl.when(pl.program_id(1) == 0)
    def _():
        state_ref[...] = jnp.zeros_like(state_ref)

    hp = hp_ref[...]
    a_head = -jnp.exp(hp[:, 0:1])
    ri = lax.broadcasted_iota(jnp.int32, (q, q), 0)
    ci = lax.broadcasted_iota(jnp.int32, (q, q), 1)
    upper_incl = (ri <= ci)
    cum_mat = upper_incl.astype(BF16)

    for sub in range(tb // q):
        cols = slice(sub * q, (sub + 1) * q)
        dt = dt_ref[:, cols]
        a = dt * a_head
        a0, a1, a2 = _split3(a)
        acs = (jnp.dot(a0, cum_mat, preferred_element_type=F32)
               + jnp.dot(a1, cum_mat, preferred_element_type=F32)
               + jnp.dot(a2, cum_mat, preferred_element_type=F32))
        last = acs[:, q - 1:q]
        acs_ref[...] = acs
        dts_ref[...] = dt
        wgt_ref[...] = jnp.exp(last - acs) * dt
        od_ref[...] = jnp.exp(acs)
        cdec_ref[...] = jnp.exp(jnp.broadcast_to(last, (n_heads, n)))

        def group_body(g, carry):
            rg = pl.multiple_of(g * n, n)
            bt = b_ref[pl.ds(rg, n), cols]
            ct = c_ref[pl.ds(rg, n), cols]
            bg = bt.T.astype(BF16)
            cbt = jnp.dot(bg, ct.astype(BF16), preferred_element_type=F32)

            for e in range(heads_per_group):
                h = g * heads_per_group + e
                r0 = pl.multiple_of(h * p, p)
                acs_row = acs_ref[pl.ds(h, 1), :]
                row_b = jnp.broadcast_to(acs_row, (q, q))
                seg = row_b - row_b.T
                lt = jnp.exp(jnp.where(upper_incl, seg, -jnp.inf))
                mt = (cbt * lt).astype(BF16)
                xt = x_ref[pl.ds(r0, p), cols]
                xdt = (xt * dts_ref[pl.ds(h, 1), :]).astype(BF16)
                prev = state_ref[h]
                odct = (ct * od_ref[pl.ds(h, 1), :]).astype(BF16)
                yt = (jnp.dot(xdt, mt, preferred_element_type=F32)
                      + jnp.dot(prev.astype(BF16), odct, preferred_element_type=F32))
                dsk = hp_ref[pl.ds(h, 1), :][:, 1:2]
                yt = yt + xt * dsk
                yg_ref[pl.ds(r0, p), cols] = yt * gz_ref[pl.ds(r0, p), cols].astype(F32)
                xw = (xt * wgt_ref[pl.ds(h, 1), :]).astype(BF16)
                state_ref[h] = (prev * cdec_ref[pl.ds(h, 1), :]
                                + jnp.dot(xw, bg, preferred_element_type=F32))
            return carry

        lax.fori_loop(0, SSD_GROUPS, group_body, 0)

    gsz = d_inner // SSD_GROUPS

    def norm_body(g, carry):
        r0 = pl.multiple_of(g * gsz, gsz)
        y = yg_ref[pl.ds(r0, gsz), :]
        ms = jnp.mean(y * y, axis=0, keepdims=True)
        y = y * lax.rsqrt(ms + GATED_NORM_EPS) * cp_ref[pl.ds(r0, gsz), :]
        o_ref[pl.ds(r0, gsz), :] = y.astype(o_ref.dtype)
        return carry

    lax.fori_loop(0, SSD_GROUPS, norm_body, 0)


def ssd_core(gz_t, xbc_t, dt_t, a_log, d_skip, norm_w, batch, seq, tb=256):
    n_heads = a_log.shape[0]
    d_inner = n_heads * SSD_HEAD_DIM
    gn = SSD_GROUPS * SSD_STATE
    t = gz_t.shape[1]
    nblk = seq // tb
    hp = jnp.zeros((n_heads, 8), F32).at[:, 0].set(a_log).at[:, 1].set(d_skip)
    cp = norm_w.reshape(d_inner, 1)

    def tmap(row_block):
        return lambda b, i: (row_block, b * nblk + i)

    const = lambda b, i: (0, 0)
    kern = functools.partial(_ssd_kernel, tb=tb, d_inner=d_inner)
    return pl.pallas_call(
        kern,
        grid=(batch, nblk),
        in_specs=[pl.BlockSpec((d_inner, tb), tmap(0)),
                  pl.BlockSpec((d_inner, tb), tmap(0)),
                  pl.BlockSpec((gn, tb), tmap(d_inner // gn)),
                  pl.BlockSpec((gn, tb), tmap(d_inner // gn + 1)),
                  pl.BlockSpec((n_heads, tb), tmap(0)),
                  pl.BlockSpec((n_heads, 8), const),
                  pl.BlockSpec((d_inner, 1), const)],
        out_specs=pl.BlockSpec((d_inner, tb), tmap(0)),
        out_shape=jax.ShapeDtypeStruct((d_inner, t), BF16),
        scratch_shapes=[pltpu.VMEM((d_inner, tb), F32),
                        pltpu.VMEM((n_heads, SSD_HEAD_DIM, SSD_STATE), F32),
                        pltpu.VMEM((n_heads, SSD_Q), F32),
                        pltpu.VMEM((n_heads, SSD_Q), F32),
                        pltpu.VMEM((n_heads, SSD_Q), F32),
                        pltpu.VMEM((n_heads, SSD_Q), F32),
                        pltpu.VMEM((n_heads, SSD_STATE), F32)],
        compiler_params=_cparams("arbitrary", "arbitrary"),
        name="ssd_core",
    )(gz_t, xbc_t, xbc_t, xbc_t, dt_t, hp, cp)


def _sb_kernel(q_ref, k_ref, v_ref, o_ref, acc_ref, *, blk, heads):
    qi = pl.program_id(2)
    dh = SB_HEAD_DIM
    ri = lax.broadcasted_iota(jnp.int32, (blk, blk), 0)
    ci = lax.broadcasted_iota(jnp.int32, (blk, blk), 1)
    strict_lower = (ri > ci)
    tail_mat = strict_lower.astype(BF16)
    sign = jnp.uint32(0x80000000)

    def tile(kb, carries, masked):
        k0 = pl.multiple_of(kb * blk, blk)
        hs = range(heads)
        cs = [slice(hh * dh, (hh + 1) * dh) for hh in hs]
        z = [lax.dot_general(q_ref[:, cs[hh]], k_ref[pl.ds(k0, blk), cs[hh]],
                             (((1,), (1,)), ((), ())), preferred_element_type=F32) for hh in hs]
        sp = []
        for hh in hs:
            neg_abs = lax.bitcast_convert_type(
                lax.bitcast_convert_type(z[hh], jnp.uint32) | sign, F32)
            s = jnp.maximum(z[hh], 0.0) + jnp.log(1.0 + jnp.exp2(neg_abs)) * LOG2E
            sp.append(jnp.where(strict_lower, s, 0.0) if masked else s)
        tail = [jnp.dot(sp[hh].astype(BF16), tail_mat, preferred_element_type=F32) for hh in hs]
        w = []
        for hh in hs:
            e = jnp.exp2(z[hh] - sp[hh] - (tail[hh] + carries[hh]))
            w.append((jnp.where(strict_lower, e, 0.0) if masked else e).astype(BF16))
        for hh in hs:
            acc_ref[hh] += jnp.dot(w[hh], v_ref[pl.ds(k0, blk), cs[hh]],
                                   preferred_element_type=F32)
        return tuple(carries[hh] + jnp.sum(sp[hh], axis=1, keepdims=True) for hh in hs)

    acc_ref[...] = jnp.zeros_like(acc_ref)
    carries = tile(qi, tuple(jnp.zeros((blk, 1), F32) for _ in range(heads)), True)
    lax.fori_loop(0, qi, lambda i, c: tile(qi - 1 - i, c, False), carries)
    for hh in range(heads):
        o_ref[:, hh * dh:(hh + 1) * dh] = acc_ref[hh].astype(o_ref.dtype)


def sb_attention(qkv, batch, seq, n_heads, blk=256, heads=8):
    t = qkv.shape[0]
    dh = SB_HEAD_DIM
    nq = seq // blk
    heads = min(heads, n_heads)
    hg = n_heads // heads
    kern = functools.partial(_sb_kernel, blk=blk, heads=heads)
    return pl.pallas_call(
        kern,
        grid=(batch, hg, nq),
        in_specs=[pl.BlockSpec((blk, heads * dh), lambda b, h, i: (b * nq + i, h)),
                  pl.BlockSpec((seq, heads * dh), lambda b, h, i: (b, hg + h)),
                  pl.BlockSpec((seq, heads * dh), lambda b, h, i: (b, 2 * hg + h))],
        out_specs=pl.BlockSpec((blk, heads * dh), lambda b, h, i: (b * nq + i, h)),
        out_shape=jax.ShapeDtypeStruct((t, n_heads * dh), BF16),
        scratch_shapes=[pltpu.VMEM((heads, blk, dh), F32)],
        compiler_params=_cparams("parallel", "parallel", "arbitrary"),
        name="sb_attention",
    )(qkv, qkv, qkv)


def _gather_rows_kernel(idx_ref, nrows_ref, src_ref, o_ref, buf_ref, sem, *, rows):
    base = pl.program_id(0) * rows

    def row_copy(r, src_row):
        return pltpu.make_async_copy(src_ref.at[pl.ds(src_row, 1), :],
                                     buf_ref.at[pl.ds(r, 1), :], sem)

    def issue(r, c):
        row_copy(r, idx_ref[base + r]).start()
        return c

    def drain(r, c):
        row_copy(r, 0).wait()
        return c

    @pl.when(base < nrows_ref[0])
    def _():
        lax.fori_loop(0, rows, issue, 0)
        lax.fori_loop(0, rows, drain, 0)
        o_ref[...] = buf_ref[...].astype(o_ref.dtype)

    @pl.when(base >= nrows_ref[0])
    def _():
        o_ref[...] = jnp.zeros_like(o_ref)


def gather_rows(src, idx, n_rows, rows=256):
    n = idx.shape[0]
    d = src.shape[1]
    return pl.pallas_call(
        functools.partial(_gather_rows_kernel, rows=rows),
        grid_spec=pltpu.PrefetchScalarGridSpec(
            num_scalar_prefetch=2,
            grid=(n // rows,),
            in_specs=[pl.BlockSpec(memory_space=pl.ANY)],
            out_specs=pl.BlockSpec((rows, d), lambda i, idx, nr: (i, 0)),
            scratch_shapes=[pltpu.VMEM((rows, d), src.dtype),
                            pltpu.SemaphoreType.DMA(())]),
        out_shape=jax.ShapeDtypeStruct((n, d), BF16),
        compiler_params=_cparams("arbitrary"),
        name="gather_rows",
    )(idx, n_rows, src)


def _expert_changed(be_ref, i):
    return (i == 0) | (be_ref[i] != be_ref[jnp.maximum(i - 1, 0)])


MOE_SUB_ROWS = 128


def _moe_up_kernel(be_ref, nb_ref, ns_ref, x_ref, wg_ref, wu_ref, o_ref, wgs_ref, wus_ref):
    i = pl.program_id(1)
    tm = x_ref.shape[0]

    @pl.when(i < nb_ref[0])
    def _():
        @pl.when(_expert_changed(be_ref, i))
        def _():
            wgs_ref[...] = wg_ref[...].astype(BF16)
            wus_ref[...] = wu_ref[...].astype(BF16)

        for k in range(1, tm // MOE_SUB_ROWS + 1):
            rows = k * MOE_SUB_ROWS

            @pl.when(ns_ref[i] == k)
            def _(rows=rows):
                a = x_ref[pl.ds(0, rows), :]
                g = jnp.dot(a, wgs_ref[...], preferred_element_type=F32)
                u = jnp.dot(a, wus_ref[...], preferred_element_type=F32)
                o_ref[pl.ds(0, rows), :] = (_silu(g) * u).astype(o_ref.dtype)
                if rows < tm:
                    o_ref[pl.ds(rows, tm - rows), :] = jnp.zeros((tm - rows, o_ref.shape[1]),
                                                                 o_ref.dtype)

    @pl.when(i >= nb_ref[0])
    def _():
        o_ref[...] = jnp.zeros_like(o_ref)


def moe_up(xs, w_gate, w_up, layer, block_expert, n_used, n_sub, tm, tn):
    pr, k = xs.shape
    f = w_gate.shape[3]
    w_spec = pl.BlockSpec((None, None, k, tn), lambda j, i, be, nb, ns: (layer, be[i], 0, j))
    return pl.pallas_call(
        _moe_up_kernel,
        grid_spec=pltpu.PrefetchScalarGridSpec(
            num_scalar_prefetch=3,
            grid=(f // tn, pr // tm),
            in_specs=[pl.BlockSpec((tm, k), lambda j, i, be, nb, ns: (i, 0)), w_spec, w_spec],
            out_specs=pl.BlockSpec((tm, tn), lambda j, i, be, nb, ns: (i, j)),
            scratch_shapes=[pltpu.VMEM((k, tn), BF16), pltpu.VMEM((k, tn), BF16)]),
        out_shape=jax.ShapeDtypeStruct((pr, f), BF16),
        compiler_params=_cparams("arbitrary", "arbitrary"),
        name="moe_up",
    )(block_expert, n_used, n_sub, xs, w_gate, w_up)


def _moe_down_kernel(be_ref, nb_ref, ns_ref, h_ref, w_ref, o_ref, ws_ref):
    i = pl.program_id(1)
    tm = h_ref.shape[0]

    @pl.when(i < nb_ref[0])
    def _():
        @pl.when(_expert_changed(be_ref, i))
        def _():
            ws_ref[...] = w_ref[...].astype(BF16)

        for k in range(1, tm // MOE_SUB_ROWS + 1):
            rows = k * MOE_SUB_ROWS

            @pl.when(ns_ref[i] == k)
            def _(rows=rows):
                o_ref[pl.ds(0, rows), :] = jnp.dot(h_ref[pl.ds(0, rows), :], ws_ref[...],
                                                   preferred_element_type=F32)
                if rows < tm:
                    o_ref[pl.ds(rows, tm - rows), :] = jnp.zeros((tm - rows, o_ref.shape[1]), F32)

    @pl.when(i >= nb_ref[0])
    def _():
        o_ref[...] = jnp.zeros_like(o_ref)


def moe_down(hid, w_down, layer, block_expert, n_used, n_sub, tm, tn):
    pr, f = hid.shape
    d = w_down.shape[3]
    return pl.pallas_call(
        _moe_down_kernel,
        grid_spec=pltpu.PrefetchScalarGridSpec(
            num_scalar_prefetch=3,
            grid=(d // tn, pr // tm),
            in_specs=[pl.BlockSpec((tm, f), lambda j, i, be, nb, ns: (i, 0)),
                      pl.BlockSpec((None, None, f, tn),
                                   lambda j, i, be, nb, ns: (layer, be[i], 0, j))],
            out_specs=pl.BlockSpec((tm, tn), lambda j, i, be, nb, ns: (i, j)),
            scratch_shapes=[pltpu.VMEM((f, tn), BF16)]),
        out_shape=jax.ShapeDtypeStruct((pr, d), F32),
        compiler_params=_cparams("arbitrary", "arbitrary"),
        name="moe_down",
    )(block_expert, n_used, n_sub, hid, w_down)


def _moe_combine_kernel(pos_ref, y_ref, h_ref, g_ref, o_ref, buf_ref, sem, *, rows):
    base = pl.program_id(0) * rows

    def row_copy(r, src_row):
        return pltpu.make_async_copy(y_ref.at[pl.ds(src_row, 1), :],
                                     buf_ref.at[pl.ds(r, 1), :], sem)

    def issue(r, c):
        for k in range(TOP_K):
            row_copy(k * rows + r, pos_ref[TOP_K * (base + r) + k]).start()
        return c

    def drain(r, c):
        row_copy(r, 0).wait()
        return c

    lax.fori_loop(0, rows, issue, 0)
    lax.fori_loop(0, TOP_K * rows, drain, 0)
    g = g_ref[...]
    acc = h_ref[...]
    for k in range(TOP_K):
        acc = acc + g[:, k:k + 1] * buf_ref[pl.ds(k * rows, rows), :]
    o_ref[...] = acc


def moe_combine(y, pos, h, gates, rows=128):
    t, d = h.shape
    return pl.pallas_call(
        functools.partial(_moe_combine_kernel, rows=rows),
        grid_spec=pltpu.PrefetchScalarGridSpec(
            num_scalar_prefetch=1,
            grid=(t // rows,),
            in_specs=[pl.BlockSpec(memory_space=pl.ANY),
                      pl.BlockSpec((rows, d), lambda i, pos: (i, 0)),
                      pl.BlockSpec((rows, TOP_K), lambda i, pos: (i, 0))],
            out_specs=pl.BlockSpec((rows, d), lambda i, pos: (i, 0)),
            scratch_shapes=[pltpu.VMEM((TOP_K * rows, d), F32),
                            pltpu.SemaphoreType.DMA(())]),
        out_shape=jax.ShapeDtypeStruct((t, d), F32),
        compiler_params=_cparams("arbitrary"),
        name="moe_combine",
    )(pos, y, h, gates)


def moe_routing(idx, tm):
    t = idx.shape[0]
    n_pairs = t * TOP_K
    nb = n_pairs // tm + N_EXPERTS
    e_flat = idx.reshape(n_pairs)
    onehot = (e_flat[:, None] == jnp.arange(N_EXPERTS)[None, :]).astype(jnp.int32)
    ranks = jnp.cumsum(onehot, axis=0) - onehot
    counts = jnp.sum(onehot, axis=0)
    blocks = (counts + tm - 1) // tm
    blk_end = jnp.cumsum(blocks)
    blk_start = blk_end - blocks
    rank = jnp.sum(ranks * onehot, axis=1)
    pos = blk_start[e_flat] * tm + rank
    src_token = jnp.zeros((nb * tm,), jnp.int32).at[pos].set(
        jnp.arange(n_pairs, dtype=jnp.int32) // TOP_K)
    n_used = blk_end[-1]
    bid = jnp.arange(nb)
    be = jnp.sum((bid[:, None] >= blk_end[None, :]).astype(jnp.int32), axis=1)
    last_e = jnp.sum((n_used - 1 >= blk_end).astype(jnp.int32))
    block_expert = jnp.where(bid < n_used, be, last_e).astype(jnp.int32)
    valid = jnp.clip(counts[block_expert] - (bid - blk_start[block_expert]) * tm, 0, tm)
    valid = jnp.where(bid < n_used, valid, 0)
    n_sub = ((valid + MOE_SUB_ROWS - 1) // MOE_SUB_ROWS).astype(jnp.int32)
    return (src_token, pos.astype(jnp.int32), block_expert,
            n_used.reshape(1).astype(jnp.int32), n_sub)


def moe_layer(h, norm_w, w_router, w_gate, w_up, w_down, layer, tm=512):
    u, gates, idx = rmsnorm_router(h, norm_w, w_router)
    src_token, pos, block_expert, n_used, n_sub = moe_routing(idx, tm)
    xs = gather_rows(u, src_token, n_used * tm)
    hid = moe_up(xs, w_gate, w_up, layer, block_expert, n_used, n_sub, tm, 512)
    y = moe_down(hid, w_down, layer, block_expert, n_used, n_sub, tm, 512)
    return moe_combine(y, pos, h, gates)


def ssd_layer(h, norm_w, w_in_t, conv_w, conv_b, dt_bias, a_log, d_skip, ssd_norm_w, w_out,
              layer, batch, seq):
    d_inner = ssd_norm_w.shape[0]
    u = rmsnorm(h, norm_w, BF16)
    gz_t, xbc_t, dt_t = ssd_in_projection(u, w_in_t, layer, conv_w, conv_b, dt_bias, d_inner, seq,
                                          tm=min(1024, seq // 2))
    g_t = ssd_core(gz_t, xbc_t, dt_t, a_log, d_skip, ssd_norm_w, batch, seq)
    return matmul(g_t, w_out, layer, F32, 1024, 512, residual=h, transposed_a=True)


def sb_layer(h, norm_w, w_qkv, w_out, layer, batch, seq):
    d = h.shape[1]
    u = rmsnorm(h, norm_w, BF16)
    qkv = matmul(u, w_qkv, layer, BF16, 1024, 512, scaled_cols=d,
                 scale=LOG2E / math.sqrt(SB_HEAD_DIM))
    o = sb_attention(qkv, batch, seq, d // SB_HEAD_DIM)
    return matmul(o, w_out, layer, F32, 1024, 512, residual=h)


def swiglu_layer(h, norm_w, w_gate, w_up, w_down, layer):
    u = rmsnorm(h, norm_w, BF16)
    hid = matmul_swiglu(u, w_gate, w_up, layer, 1024, 512)
    return matmul(hid, w_down, layer, F32, 512, 512, residual=h)


def kernel(x, norm_mix_w, norm_ffn_w, norm_final_w, ssd_w_in, ssd_conv_w, ssd_conv_b, ssd_dt_bias, ssd_A_log, ssd_D, ssd_norm_w, ssd_w_out, sb_w_qkv, sb_w_out, ffn_w_gate, ffn_w_up, ffn_w_down, moe_w_router, moe_w_gate, moe_w_up, moe_w_down):
    batch, seq, d = x.shape
    depth = norm_mix_w.shape[0]
    h = x.reshape(batch * seq, d)
    ssd_w_in_t = jnp.swapaxes(ssd_w_in, 1, 2)
    for i in range(depth):
        j = i // 2
        if i % 2 == 0:
            h = ssd_layer(h, norm_mix_w[i], ssd_w_in_t, ssd_conv_w[j], ssd_conv_b[j],
                          ssd_dt_bias[j], ssd_A_log[j], ssd_D[j], ssd_norm_w[j], ssd_w_out,
                          j, batch, seq)
            h = swiglu_layer(h, norm_ffn_w[i], ffn_w_gate, ffn_w_up, ffn_w_down, j)
        else:
            h = sb_layer(h, norm_mix_w[i], sb_w_qkv, sb_w_out, j, batch, seq)
            h = moe_layer(h, norm_ffn_w[i], moe_w_router[j], moe_w_gate, moe_w_up,
                          moe_w_down, j)
    return rmsnorm(h, norm_final_w, F32).reshape(batch, seq, d)
```
